```python
import math
import jax, jax.numpy as jnp
from jax import lax
import numpy as np

D_MODEL = 2048
BATCH = 2
SEQ = 16384
DEPTH = 2

N_MIXERS = 2
N_A_LAYERS = (DEPTH + 1) // 2
N_B_LAYERS = DEPTH // 2
GRID_W = 64
EPS = 1e-6
A_WIDTH = D_MODEL
A_CHUNK = 128
A_GROUPS = 16
A_GROUP_DIM = A_WIDTH // A_GROUPS
HEAD_DIM = 128
N_HEADS = D_MODEL // HEAD_DIM
N_KV_HEADS = 4
GQA_GROUP = N_HEADS // N_KV_HEADS
Q_BLOCK = 128
ROPE_THETA = 10000.0
ROPE_AXIS_DIM = HEAD_DIM // 2
D_FF = 5632
CONV_W = 3

kernel_name = "hybrid_gmlp_gqa_convffn_encoder"


def _rmsnorm(x, gain):
    x32 = x.astype(jnp.float32)
    y = x32 * lax.rsqrt(jnp.mean(x32 * x32, axis=-1, keepdims=True) + EPS)
    return (y * gain.astype(jnp.float32)).astype(x.dtype)


def _gmlp_mixer(h, w_in, g_v, w_s, b_s, w_out):
    B, S, _ = h.shape
    n_chunks = S // A_CHUNK
    z = jax.nn.gelu(h @ w_in, approximate=False)
    u, v = jnp.split(z, 2, axis=-1)
    v = _rmsnorm(v, g_v)
    v = v.reshape(B, n_chunks, A_CHUNK, A_GROUPS, A_GROUP_DIM)
    sv = jnp.einsum("gij,bnjgc->bnigc", w_s, v)
    sv = sv + jnp.transpose(b_s)[None, None, :, :, None]
    y = u * sv.reshape(B, S, A_WIDTH)
    return y @ w_out


def _axial_rope_tables(S):
    rows = S // GRID_W
    row_pos = jnp.broadcast_to(jnp.arange(rows)[:, None], (rows, GRID_W)).reshape(S)
    col_pos = jnp.broadcast_to(jnp.arange(GRID_W)[None, :], (rows, GRID_W)).reshape(S)
    inv_freq = ROPE_THETA ** (-jnp.arange(0, ROPE_AXIS_DIM, 2, dtype=jnp.float32) / ROPE_AXIS_DIM)
    ang_r = row_pos.astype(jnp.float32)[:, None] * inv_freq[None, :]
    ang_c = col_pos.astype(jnp.float32)[:, None] * inv_freq[None, :]
    return jnp.cos(ang_r), jnp.sin(ang_r), jnp.cos(ang_c), jnp.sin(ang_c)


def _rotate(xh, cos, sin):
    S = xh.shape[1]
    shp = (1, S) + (1,) * (xh.ndim - 3) + (cos.shape[-1],)
    cos = cos.reshape(shp)
    sin = sin.reshape(shp)
    x1, x2 = jnp.split(xh, 2, axis=-1)
    return jnp.concatenate([x1 * cos - x2 * sin, x2 * cos + x1 * sin], axis=-1)


def _apply_axial_rope(t, tables):
    cr, sr, cc, sc = tables
    t32 = t.astype(jnp.float32)
    t_row, t_col = jnp.split(t32, 2, axis=-1)
    out = jnp.concatenate([_rotate(t_row, cr, sr), _rotate(t_col, cc, sc)], axis=-1)
    return out.astype(t.dtype)


def _gqa_mixer(h, w_qkv, g_q, g_k, w_o):
    B, S, _ = h.shape
    qkv = h @ w_qkv
    q, k, v = jnp.split(qkv, [N_HEADS * HEAD_DIM, (N_HEADS + N_KV_HEADS) * HEAD_DIM], axis=-1)
    q = q.reshape(B, S, N_KV_HEADS, GQA_GROUP, HEAD_DIM)
    k = k.reshape(B, S, N_KV_HEADS, HEAD_DIM)
    v = v.reshape(B, S, N_KV_HEADS, HEAD_DIM)
    q = _rmsnorm(q, g_q)
    k = _rmsnorm(k, g_k)
    tables = _axial_rope_tables(S)
    q = _apply_axial_rope(q, tables)
    k = _apply_axial_rope(k, tables)
    scale = 1.0 / math.sqrt(HEAD_DIM)
    n_blk = S // Q_BLOCK
    qb = q.reshape(B, n_blk, Q_BLOCK, N_KV_HEADS, GQA_GROUP, HEAD_DIM).transpose(1, 0, 2, 3, 4, 5)

    def attend(q_blk):
        s = jnp.einsum("bqhgd,bkhd->bhgqk", q_blk, k, preferred_element_type=jnp.float32) * scale
        p = jax.nn.softmax(s, axis=-1).astype(v.dtype)
        return jnp.einsum("bhgqk,bkhd->bqhgd", p, v)

    o = lax.map(attend, qb)
    o = o.transpose(1, 0, 2, 3, 4, 5).reshape(B, S, N_HEADS * HEAD_DIM)
    return o @ w_o


def _conv_ffn(h, w_up, conv_w, conv_b, w_down):
    a = h @ w_up
    S = a.shape[1]
    ap = jnp.pad(a, ((0, 0), (1, 1), (0, 0)))
    a = (conv_w[0] * ap[:, 0:S] + conv_w[1] * ap[:, 1:S + 1] + conv_w[2] * ap[:, 2:S + 2]
         + conv_b)
    g, val = jnp.split(a, 2, axis=-1)
    return (jax.nn.gelu(g, approximate=False) * val) @ w_down


def setup_inputs(seed: int = 0) -> dict:
    key = jax.random.key(seed)
    ks = jax.random.split(key, 24)
    f32 = jnp.float32

    def nrm(k, shape, fan_in, mult=1.0):
        return jax.random.normal(k, shape, f32) * (mult * fan_in ** -0.5)

    D = D_MODEL
    qkv_out = (N_HEADS + 2 * N_KV_HEADS) * HEAD_DIM
    return {
        "x": jax.random.normal(ks[0], (BATCH, SEQ, D), f32),
        "c": jax.random.normal(ks[1], (BATCH, D), f32),
        "w_ada": nrm(ks[2], (DEPTH, D, 6 * D), D, 0.5),
        "b_ada": 0.02 * jax.random.normal(ks[3], (DEPTH, 6 * D), f32),
        "g_norm": 1.0 + 0.02 * jax.random.normal(ks[4], (DEPTH, 2, D), f32),
        "g_final": 1.0 + 0.02 * jax.random.normal(ks[5], (D,), f32),
        "a_w_in": nrm(ks[6], (N_A_LAYERS, D, 2 * A_WIDTH), D),
        "a_g_v": 1.0 + 0.02 * jax.random.normal(ks[7], (N_A_LAYERS, A_WIDTH), f32),
        "a_w_s": nrm(ks[8], (N_A_LAYERS, A_GROUPS, A_CHUNK, A_CHUNK), A_CHUNK),
        "a_b_s": 1.0 + 0.02 * jax.random.normal(ks[9], (N_A_LAYERS, A_GROUPS, A_CHUNK), f32),
        "a_w_out": nrm(ks[10], (N_A_LAYERS, A_WIDTH, D), A_WIDTH),
        "b_w_qkv": nrm(ks[11], (N_B_LAYERS, D, qkv_out), D),
        "b_g_q": 1.0 + 0.02 * jax.random.normal(ks[12], (N_B_LAYERS, HEAD_DIM), f32),
        "b_g_k": 1.0 + 0.02 * jax.random.normal(ks[13], (N_B_LAYERS, HEAD_DIM), f32),
        "b_w_o": nrm(ks[14], (N_B_LAYERS, N_HEADS * HEAD_DIM, D), N_HEADS * HEAD_DIM),
        "f_w_up": nrm(ks[15], (DEPTH, D, 2 * D_FF), D),
        "f_conv_w": nrm(ks[16], (DEPTH, CONV_W, 2 * D_FF), CONV_W),
        "f_conv_b": 0.02 * jax.random.normal(ks[17], (DEPTH, 2 * D_FF), f32),
        "f_w_down": nrm(ks[18], (DEPTH, D_FF, D), D_FF),
    }


def reference(x, c, w_ada, b_ada, g_norm, g_final,
              a_w_in, a_g_v, a_w_s, a_b_s, a_w_out,
              b_w_qkv, b_g_q, b_g_k, b_w_o,
              f_w_up, f_conv_w, f_conv_b, f_w_down):
    cond = jax.nn.silu(c)
    for i in range(DEPTH):
        mod = cond @ w_ada[i] + b_ada[i]
        sh1, sc1, gt1, sh2, sc2, gt2 = jnp.split(mod[:, None, :], 6, axis=-1)
        h = _rmsnorm(x, g_norm[i, 0]) * (1.0 + sc1) + sh1
        j = i // N_MIXERS
        if i % N_MIXERS == 0:
            y = _gmlp_mixer(h, a_w_in[j], a_g_v[j], a_w_s[j], a_b_s[j], a_w_out[j])
        else:
            y = _gqa_mixer(h, b_w_qkv[j], b_g_q[j], b_g_k[j], b_w_o[j])
        x = x + gt1 * y
        h = _rmsnorm(x, g_norm[i, 1]) * (1.0 + sc2) + sh2
        x = x + gt2 * _conv_ffn(h, f_w_up[i], f_conv_w[i], f_conv_b[i], f_w_down[i])
    return _rmsnorm(x, g_final)
```

```python
import functools
import math

import jax
import jax.numpy as jnp
from jax import lax
from jax.experimental import pallas as pl
from jax.experimental.pallas import tpu as pltpu

EPS = 1e-6
GRID_W = 64
ROPE_THETA = 10000.0
V7X_VMEM_BYTES = 64 * 1024 * 1024
V7X_SUBLANES = 8
VMEM_LIMIT_BYTES = V7X_VMEM_BYTES - 8 * 1024 * 1024

BF16 = jnp.bfloat16
F32 = jnp.float32


def _compiler_params(n_grid_axes):
    return pltpu.CompilerParams(
        dimension_semantics=("arbitrary",) * n_grid_axes,
        vmem_limit_bytes=VMEM_LIMIT_BYTES,
    )


def _rms_scale(x):
    return lax.rsqrt(jnp.mean(x * x, axis=-1, keepdims=True) + EPS)


def _modnorm(x, gain, scale, shift):
    return (x * _rms_scale(x)) * (gain * (1.0 + scale)) + shift


def _gelu(x):
    return 0.5 * x * (1.0 + lax.erf(x * (1.0 / math.sqrt(2.0))))


def _adaln_kernel(c_ref, w_ref, b_ref, o_ref):
    c = c_ref[...]
    cond = c * (1.0 / (1.0 + jnp.exp(-c)))
    o_ref[...] = jnp.dot(cond, w_ref[...], precision=lax.Precision.HIGHEST,
                         preferred_element_type=F32) + b_ref[...]


def _adaln(c, w_ada, b_ada, tn=1024):
    depth, d, n = w_ada.shape
    bsz = c.shape[0]
    rows = -(-bsz // V7X_SUBLANES) * V7X_SUBLANES
    c_pad = jnp.zeros((rows, d), F32).at[:bsz].set(c)
    out = pl.pallas_call(
        _adaln_kernel,
        grid=(depth, n // tn),
        in_specs=[
            pl.BlockSpec((rows, d), lambda l, j: (0, 0)),
            pl.BlockSpec((None, d, tn), lambda l, j: (l, 0, j)),
            pl.BlockSpec((None, 1, tn), lambda l, j: (l, 0, j)),
        ],
        out_specs=pl.BlockSpec((None, rows, tn), lambda l, j: (l, 0, j)),
        out_shape=jax.ShapeDtypeStruct((depth, rows, n), F32),
        compiler_params=_compiler_params(2),
        name="adaln",
    )(c_pad, w_ada, b_ada.reshape(depth, 1, n))
    return out[:, :bsz].reshape(depth, bsz, 6, d)


def _gmlp_in_kernel(x_ref, mod_ref, g_ref, w_ref, gv_ref, u_ref, vn_ref, h_scr, v_scr, *, nc):
    width = u_ref.shape[-1]
    h_scr[...] = _modnorm(x_ref[...], g_ref[...], mod_ref[1:2, :], mod_ref[0:1, :]).astype(BF16)
    ssq = jnp.zeros((x_ref.shape[0], 1), F32)
    for c in range(width // nc):
        cols = slice(c * nc, (c + 1) * nc)
        zu = _gelu(jnp.dot(h_scr[...], w_ref[:, cols], preferred_element_type=F32))
        u_ref[:, cols] = zu.astype(BF16)
        zv = _gelu(jnp.dot(h_scr[...], w_ref[:, width + c * nc: width + (c + 1) * nc],
                           preferred_element_type=F32))
        v_scr[:, cols] = zv
        ssq = ssq + jnp.sum(zv * zv, axis=-1, keepdims=True)
    r = lax.rsqrt(ssq * (1.0 / width) + EPS)
    vn_ref[...] = ((v_scr[...] * r) * gv_ref[...]).astype(BF16)


def _gmlp_in(x, mod, g, w_in, g_v, tm=512, nc=512):
    bsz, s, d = x.shape
    width = w_in.shape[1] // 2
    kernel = functools.partial(_gmlp_in_kernel, nc=nc)
    return pl.pallas_call(
        kernel,
        grid=(bsz, s // tm),
        in_specs=[
            pl.BlockSpec((None, tm, d), lambda b, i: (b, i, 0)),
            pl.BlockSpec((None, 6, d), lambda b, i: (b, 0, 0)),
            pl.BlockSpec((1, d), lambda b, i: (0, 0)),
            pl.BlockSpec((d, 2 * width), lambda b, i: (0, 0)),
            pl.BlockSpec((1, width), lambda b, i: (0, 0)),
        ],
        out_specs=[
            pl.BlockSpec((None, tm, width), lambda b, i: (b, i, 0)),
            pl.BlockSpec((None, tm, width), lambda b, i: (b, i, 0)),
        ],
        out_shape=[jax.ShapeDtypeStruct((bsz, s, width), BF16)] * 2,
        scratch_shapes=[pltpu.VMEM((tm, d), BF16), pltpu.VMEM((tm, width), F32)],
        compiler_params=_compiler_params(2),
        name="gmlp_in",
    )(x, mod, g.reshape(1, d), w_in, g_v.reshape(1, width))


def _gmlp_out_kernel(u_ref, vn_ref, ws_ref, bs_ref, x_ref, mod_ref, wo_ref, o_ref, y_scr):
    n_groups, chunk, _ = ws_ref.shape
    gd = u_ref.shape[-1] // n_groups
    for n in range(u_ref.shape[0] // chunk):
        rows = slice(n * chunk, (n + 1) * chunk)
        for g in range(n_groups):
            cols = slice(g * gd, (g + 1) * gd)
            sv = jnp.dot(ws_ref[g], vn_ref[rows, cols], preferred_element_type=F32) + bs_ref[:, cols]
            y_scr[rows, cols] = (u_ref[rows, cols].astype(F32) * sv).astype(BF16)
    y = jnp.dot(y_scr[...], wo_ref[...], preferred_element_type=F32)
    o_ref[...] = x_ref[...] + mod_ref[2:3, :] * y


def _gmlp_out(u, vn, w_s, bs_full, x, mod, w_out, tm=512):
    bsz, s, d = x.shape
    width = u.shape[-1]
    n_groups, chunk, _ = w_s.shape
    return pl.pallas_call(
        _gmlp_out_kernel,
        grid=(bsz, s // tm),
        in_specs=[
            pl.BlockSpec((None, tm, width), lambda b, i: (b, i, 0)),
            pl.BlockSpec((None, tm, width), lambda b, i: (b, i, 0)),
            pl.BlockSpec((n_groups, chunk, chunk), lambda b, i: (0, 0, 0)),
            pl.BlockSpec((chunk, width), lambda b, i: (0, 0)),
            pl.BlockSpec((None, tm, d), lambda b, i: (b, i, 0)),
            pl.BlockSpec((None, 6, d), lambda b, i: (b, 0, 0)),
            pl.BlockSpec((width, d), lambda b, i: (0, 0)),
        ],
        out_specs=pl.BlockSpec((None, tm, d), lambda b, i: (b, i, 0)),
        out_shape=jax.ShapeDtypeStruct((bsz, s, d), F32),
        scratch_shapes=[pltpu.VMEM((tm, width), BF16)],
        compiler_params=_compiler_params(2),
        name="gmlp_out",
    )(u, vn, w_s, bs_full, x, mod, w_out)


def _rope(t, cos, sin_signed, half_idx):
    quarter = t.shape[-1] // 4
    partner = jnp.where(half_idx, pltpu.roll(t, t.shape[-1] - quarter, 1), pltpu.roll(t, quarter, 1))
    return t * cos + partner * sin_signed


def _qkv_kernel(x_ref, mod_ref, g_ref, wqk_ref, wvt_ref, gq_ref, gk_ref, cos_ref, sin_ref,
                q_ref, k_ref, vt_ref, h_scr, *, q_scale):
    n_kv, group, tm, hd = q_ref.shape
    h_scr[...] = _modnorm(x_ref[...], g_ref[...], mod_ref[1:2, :], mod_ref[0:1, :]).astype(BF16)
    cos = cos_ref[...]
    sin = sin_ref[...]
    lane = lax.broadcasted_iota(jnp.int32, (tm, hd), 1)
    half_idx = (lane % (hd // 2)) < (hd // 4)

    def head(t, gain, post):
        t = t * lax.rsqrt(jnp.mean(t * t, axis=-1, keepdims=True) + EPS) * gain
        return (_rope(t, cos, sin, half_idx) * post).astype(BF16)

    for kv in range(n_kv):
        cols = slice(kv * group * hd, (kv + 1) * group * hd)
        tq = jnp.dot(h_scr[...], wqk_ref[:, cols], preferred_element_type=F32)
        for g in range(group):
            q_ref[kv, g] = head(tq[:, g * hd:(g + 1) * hd], gq_ref[...], q_scale)
    k_off = n_kv * group * hd
    tk = jnp.dot(h_scr[...], wqk_ref[:, k_off:k_off + n_kv * hd], preferred_element_type=F32)
    for kv in range(n_kv):
        k_ref[kv] = head(tk[:, kv * hd:(kv + 1) * hd], gk_ref[...], 1.0)
    vt = lax.dot_general(wvt_ref[...], h_scr[...], (((1,), (1,)), ((), ())),
                         preferred_element_type=F32)
    for kv in range(n_kv):
        vt_ref[kv, 0] = vt[kv * hd:(kv + 1) * hd, :].astype(BF16)


def _qkv(x, mod, g, w_qk, w_vt, g_q, g_k, cos, sin, n_kv, tm=512):
    bsz, s, d = x.shape
    hd = g_q.shape[-1]
    n_heads = d // hd
    group = n_heads // n_kv
    kernel = functools.partial(_qkv_kernel, q_scale=1.0 / math.sqrt(hd))
    return pl.pallas_call(
        kernel,
        grid=(bsz, s // tm),
        in_specs=[
            pl.BlockSpec((None, tm, d), lambda b, i: (b, i, 0)),
            pl.BlockSpec((None, 6, d), lambda b, i: (b, 0, 0)),
            pl.BlockSpec((1, d), lambda b, i: (0, 0)),
            pl.BlockSpec(w_qk.shape, lambda b, i: (0, 0)),
            pl.BlockSpec(w_vt.shape, lambda b, i: (0, 0)),
            pl.BlockSpec((1, hd), lambda b, i: (0, 0)),
            pl.BlockSpec((1, hd), lambda b, i: (0, 0)),
            pl.BlockSpec((tm, hd), lambda b, i: (i, 0)),
            pl.BlockSpec((tm, hd), lambda b, i: (i, 0)),
        ],
        out_specs=[
            pl.BlockSpec((None, n_kv, group, tm, hd), lambda b, i: (b, 0, 0, i, 0)),
            pl.BlockSpec((None, n_kv, tm, hd), lambda b, i: (b, 0, i, 0)),
            pl.BlockSpec((None, n_kv, 1, hd, tm), lambda b, i: (b, 0, i, 0, 0)),
        ],
        out_shape=[
            jax.ShapeDtypeStruct((bsz, n_kv, group, s, hd), BF16),
            jax.ShapeDtypeStruct((bsz, n_kv, s, hd), BF16),
            jax.ShapeDtypeStruct((bsz, n_kv, s // tm, hd, tm), BF16),
        ],
        scratch_shapes=[pltpu.VMEM((tm, d), BF16)],
        compiler_params=_compiler_params(2),
        name="qkv",
    )(x, mod, g.reshape(1, d), w_qk, w_vt, g_q.reshape(1, hd), g_k.reshape(1, hd), cos, sin)


def _attn_kernel(q_ref, k_ref, vt_ref, o_ref, acc_scr):
    group, tq, hd = q_ref.shape
    n_kt, _, tk = vt_ref.shape
    nq = group * tq
    q = q_ref[...].reshape(nq, hd)
    acc_scr[...] = jnp.zeros_like(acc_scr)

    def step(kt, carry):
        m, l = carry
        k_tile = k_ref[pl.ds(pl.multiple_of(kt * tk, tk), tk), :]
        s = lax.dot_general(k_tile, q, (((1,), (1,)), ((), ())), preferred_element_type=F32)
        m_new = jnp.maximum(m, jnp.max(s, axis=0, keepdims=True))
        alpha = jnp.exp(m - m_new)
        p = jnp.exp(s - m_new)
        l = alpha * l + jnp.sum(p, axis=0, keepdims=True)
        acc_scr[...] = alpha * acc_scr[...] + jnp.dot(vt_ref[kt], p.astype(BF16),
                                                      preferred_element_type=F32)
        return m_new, l

    m0 = jnp.full((1, nq), -jnp.inf, F32)
    l0 = jnp.zeros((1, nq), F32)
    _, l = lax.fori_loop(0, n_kt, step, (m0, l0))
    out_t = acc_scr[...] / l
    for g in range(group):
        o_ref[:, g * hd:(g + 1) * hd] = out_t[:, g * tq:(g + 1) * tq].T.astype(BF16)


def _attention(q, k, vt, tq=128):
    bsz, n_kv, group, s, hd = q.shape
    n_kt, tk = vt.shape[2], vt.shape[4]
    return pl.pallas_call(
        _attn_kernel,
        grid=(bsz, n_kv, s // tq),
        in_specs=[
            pl.BlockSpec((None, None, group, tq, hd), lambda b, h, i: (b, h, 0, i, 0)),
            pl.BlockSpec((None, None, s, hd), lambda b, h, i: (b, h, 0, 0)),
            pl.BlockSpec((None, None, n_kt, hd, tk), lambda b, h, i: (b, h, 0, 0, 0)),
        ],
        out_specs=pl.BlockSpec((None, tq, group * hd), lambda b, h, i: (b, i, h)),
        out_shape=jax.ShapeDtypeStruct((bsz, s, n_kv * group * hd), BF16),
        scratch_shapes=[pltpu.VMEM((hd, group * tq), F32)],
        compiler_params=_compiler_params(3),
        name="attention",
    )(q, k, vt)


def _oproj_kernel(a_ref, w_ref, x_ref, mod_ref, o_ref):
    y = jnp.dot(a_ref[...], w_ref[...], preferred_element_type=F32)
    o_ref[...] = x_ref[...] + mod_ref[2:3, :] * y


def _oproj(a, w_o, x, mod, tm=512):
    bsz, s, d = x.shape
    return pl.pallas_call(
        _oproj_kernel,
        grid=(bsz, s // tm),
        in_specs=[
            pl.BlockSpec((None, tm, a.shape[-1]), lambda b, i: (b, i, 0)),
            pl.BlockSpec(w_o.shape, lambda b, i: (0, 0)),
            pl.BlockSpec((None, tm, d), lambda b, i: (b, i, 0)),
            pl.BlockSpec((None, 6, d), lambda b, i: (b, 0, 0)),
        ],
        out_specs=pl.BlockSpec((None, tm, d), lambda b, i: (b, i, 0)),
        out_shape=jax.ShapeDtypeStruct((bsz, s, d), F32),
        compiler_params=_compiler_params(2),
        name="oproj",
    )(a, w_o, x, mod)


def _ffn_kernel(x_ref, xp_ref, xn_ref, mod_ref, g_ref, wg_ref, wv_ref, cwg_ref, cwv_ref,
                cbg_ref, cbv_ref, wd_ref, gf_ref, o_ref, h_scr, ag_scr, av_scr, *, final_norm):
    tm = x_ref.shape[0]
    halo = xp_ref.shape[0]
    i = pl.program_id(1)
    j = pl.program_id(2)

    @pl.when(j == 0)
    def _():
        gain, scale, shift = g_ref[...], mod_ref[4:5, :], mod_ref[3:4, :]
        h_scr[halo:halo + tm, :] = _modnorm(x_ref[...], gain, scale, shift).astype(BF16)
        hp = _modnorm(xp_ref[...], gain, scale, shift)
        hn = _modnorm(xn_ref[...], gain, scale, shift)
        h_scr[0:halo, :] = jnp.where(i == 0, 0.0, hp).astype(BF16)
        h_scr[halo + tm:, :] = jnp.where(i == pl.num_programs(1) - 1, 0.0, hn).astype(BF16)
        o_ref[...] = jnp.zeros_like(o_ref)

    def conv(a_scr, cw_ref, cb_ref):
        return (cw_ref[0:1, :] * a_scr[halo - 1:halo - 1 + tm, :]
                + cw_ref[1:2, :] * a_scr[halo:halo + tm, :]
                + cw_ref[2:3, :] * a_scr[halo + 1:halo + 1 + tm, :]
                + cb_ref[...])

    ag_scr[...] = jnp.dot(h_scr[...], wg_ref[...], preferred_element_type=F32)
    av_scr[...] = jnp.dot(h_scr[...], wv_ref[...], preferred_element_type=F32)
    act = (_gelu(conv(ag_scr, cwg_ref, cbg_ref)) * conv(av_scr, cwv_ref, cbv_ref)).astype(BF16)
    o_ref[...] += jnp.dot(act, wd_ref[...], preferred_element_type=F32)

    @pl.when(j == pl.num_programs(2) - 1)
    def _():
        y = x_ref[...] + mod_ref[5:6, :] * o_ref[...]
        if final_norm:
            y = (y * _rms_scale(y)) * gf_ref[...]
        o_ref[...] = y


def _ffn(x, mod, g, w_up, conv_w, conv_b, w_down, g_final, final_norm, tm=512, tf=512):
    bsz, s, d = x.shape
    dff = w_down.shape[0]
    nj = dff // tf
    halo = V7X_SUBLANES
    nhb = tm // halo
    last_hb = s // halo - 1
    kernel = functools.partial(_ffn_kernel, final_norm=final_norm)
    return pl.pallas_call(
        kernel,
        grid=(bsz, s // tm, nj),
        in_specs=[
            pl.BlockSpec((None, tm, d), lambda b, i, j: (b, i, 0)),
            pl.BlockSpec((None, halo, d), lambda b, i, j: (b, jnp.maximum(i * nhb - 1, 0), 0)),
            pl.BlockSpec((None, halo, d), lambda b, i, j: (b, jnp.minimum((i + 1) * nhb, last_hb), 0)),
            pl.BlockSpec((None, 6, d), lambda b, i, j: (b, 0, 0)),
            pl.BlockSpec((1, d), lambda b, i, j: (0, 0)),
            pl.BlockSpec((d, tf), lambda b, i, j: (0, j)),
            pl.BlockSpec((d, tf), lambda b, i, j: (0, nj + j)),
            pl.BlockSpec((3, tf), lambda b, i, j: (0, j)),
            pl.BlockSpec((3, tf), lambda b, i, j: (0, nj + j)),
            pl.BlockSpec((1, tf), lambda b, i, j: (0, j)),
            pl.BlockSpec((1, tf), lambda b, i, j: (0, nj + j)),
            pl.BlockSpec((tf, d), lambda b, i, j: (j, 0)),
            pl.BlockSpec((1, d), lambda b, i, j: (0, 0)),
        ],
        out_specs=pl.BlockSpec((None, tm, d), lambda b, i, j: (b, i, 0)),
        out_shape=jax.ShapeDtypeStruct((bsz, s, d), F32),
        scratch_shapes=[
            pltpu.VMEM((tm + 2 * halo, d), BF16),
            pltpu.VMEM((tm + 2 * halo, tf), F32),
            pltpu.VMEM((tm + 2 * halo, tf), F32),
        ],
        compiler_params=_compiler_params(3),
        name="ffn",
    )(x, x, x, mod, g.reshape(1, d), w_up, w_up, conv_w, conv_w,
      conv_b.reshape(1, -1), conv_b.reshape(1, -1), w_down, g_final.reshape(1, d))


def _rope_tables(s, hd):
    axis_dim = hd // 2
    pos = jnp.arange(s)
    inv_freq = ROPE_THETA ** (-jnp.arange(0, axis_dim, 2, dtype=F32) / axis_dim)
    ang_r = (pos // GRID_W).astype(F32)[:, None] * inv_freq[None, :]
    ang_c = (pos % GRID_W).astype(F32)[:, None] * inv_freq[None, :]
    cos = jnp.concatenate([jnp.cos(ang_r)] * 2 + [jnp.cos(ang_c)] * 2, axis=-1)
    sin = jnp.concatenate([-jnp.sin(ang_r), jnp.sin(ang_r), -jnp.sin(ang_c), jnp.sin(ang_c)], axis=-1)
    return cos, sin


def kernel(x, c, w_ada, b_ada, g_norm, g_final, a_w_in, a_g_v, a_w_s, a_b_s, a_w_out,
           b_w_qkv, b_g_q, b_g_k, b_w_o, f_w_up, f_conv_w, f_conv_b, f_w_down):
    depth = w_ada.shape[0]
    bsz, s, d = x.shape
    hd = b_g_q.shape[-1]
    n_heads = d // hd
    n_kv = (b_w_qkv.shape[-1] // hd - n_heads) // 2
    n_mixers = 2

    mod = _adaln(c, w_ada, b_ada)
    cos, sin = _rope_tables(s, hd)

    for i in range(depth):
        j = i // n_mixers
        if i % n_mixers == 0:
            n_groups, chunk, _ = a_w_s.shape[1:]
            width = a_w_out.shape[1]
            bs_full = jnp.repeat(jnp.transpose(a_b_s[j]), width // n_groups, axis=1)
            u, vn = _gmlp_in(x, mod[i], g_norm[i, 0], a_w_in[j].astype(BF16), a_g_v[j])
            x = _gmlp_out(u, vn, a_w_s[j].astype(BF16), bs_full, x, mod[i], a_w_out[j].astype(BF16))
        else:
            w = b_w_qkv[j]
            qk_cols = (n_heads + n_kv) * hd
            q, k, vt = _qkv(x, mod[i], g_norm[i, 0], w[:, :qk_cols].astype(BF16),
                            jnp.transpose(w[:, qk_cols:]).astype(BF16), b_g_q[j], b_g_k[j],
                            cos, sin, n_kv)
            a = _attention(q, k, vt)
            x = _oproj(a, b_w_o[j].astype(BF16), x, mod[i])
        x = _ffn(x, mod[i], g_norm[i, 1], f_w_up[i].astype(BF16), f_conv_w[i], f_conv_b[i],
                 f_w_down[i].astype(BF16), g_final, final_norm=(i == depth - 1))
    return x
```

```python
import functools
import math

import jax
import jax.numpy as jnp
from jax import lax
from jax.experimental import pallas as pl
from jax.experimental.pallas import tpu as pltpu

EPS = 1e-6
GRID_W = 64
ROPE_THETA = 10000.0
V7X_VMEM_BYTES = 64 * 1024 * 1024
V7X_SUBLANES = 8
VMEM_LIMIT_BYTES = V7X_VMEM_BYTES - 8 * 1024 * 1024

BF16 = jnp.bfloat16
F32 = jnp.float32


def _compiler_params(n_grid_axes):
    return pltpu.CompilerParams(
        dimension_semantics=("arbitrary",) * n_grid_axes,
        vmem_limit_bytes=VMEM_LIMIT_BYTES,
    )


def _rms_scale(x):
    return lax.rsqrt(jnp.mean(x * x, axis=-1, keepdims=True) + EPS)


def _modnorm(x, gain, scale, shift):
    return (x * _rms_scale(x)) * (gain * (1.0 + scale)) + shift


def _gelu(x):
    return 0.5 * x * (1.0 + lax.erf(x * (1.0 / math.sqrt(2.0))))


def _adaln_kernel(c_ref, w_ref, b_ref, o_ref):
    c = c_ref[...]
    cond = c * (1.0 / (1.0 + jnp.exp(-c)))
    o_ref[...] = jnp.dot(cond, w_ref[...], precision=lax.Precision.HIGHEST,
                         preferred_element_type=F32) + b_ref[...]


def _adaln(c, w_ada, b_ada, tn=1024):
    depth, d, n = w_ada.shape
    bsz = c.shape[0]
    rows = -(-bsz // V7X_SUBLANES) * V7X_SUBLANES
    c_pad = jnp.zeros((rows, d), F32).at[:bsz].set(c)
    out = pl.pallas_call(
        _adaln_kernel,
        grid=(depth, n // tn),
        in_specs=[
            pl.BlockSpec((rows, d), lambda l, j: (0, 0)),
            pl.BlockSpec((None, d, tn), lambda l, j: (l, 0, j)),
            pl.BlockSpec((None, 1, tn), lambda l, j: (l, 0, j)),
        ],
        out_specs=pl.BlockSpec((None, rows, tn), lambda l, j: (l, 0, j)),
        out_shape=jax.ShapeDtypeStruct((depth, rows, n), F32),
        compiler_params=_compiler_params(2),
        name="adaln",
    )(c_pad, w_ada, b_ada.reshape(depth, 1, n))
    return out[:, :bsz].reshape(depth, bsz, 6, d)


def _gmlp_in_kernel(x_ref, mod_ref, g_ref, w_ref, gv_ref, u_ref, vn_ref, h_scr, v_scr, *, nc):
    width = u_ref.shape[-1]
    h_scr[...] = _modnorm(x_ref[...], g_ref[...], mod_ref[1:2, :], mod_ref[0:1, :]).astype(BF16)
    ssq = jnp.zeros((x_ref.shape[0], 1), F32)
    for c in range(width // nc):
        cols = slice(c * nc, (c + 1) * nc)
        zu = _gelu(jnp.dot(h_scr[...], w_ref[:, cols], preferred_element_type=F32))
        u_ref[:, cols] = zu.astype(BF16)
        zv = _gelu(jnp.dot(h_scr[...], w_ref[:, width + c * nc: width + (c + 1) * nc],
                           preferred_element_type=F32))
        v_scr[:, cols] = zv
        ssq = ssq + jnp.sum(zv * zv, axis=-1, keepdims=True)
    r = lax.rsqrt(ssq * (1.0 / width) + EPS)
    vn_ref[...] = ((v_scr[...] * r) * gv_ref[...]).astype(BF16)


def _gmlp_in(x, mod, g, w_in, g_v, tm=512, nc=512):
    bsz, s, d = x.shape
    width = w_in.shape[1] // 2
    kernel = functools.partial(_gmlp_in_kernel, nc=nc)
    return pl.pallas_call(
        kernel,
        grid=(bsz, s // tm),
        in_specs=[
            pl.BlockSpec((None, tm, d), lambda b, i: (b, i, 0)),
            pl.BlockSpec((None, 6, d), lambda b, i: (b, 0, 0)),
            pl.BlockSpec((1, d), lambda b, i: (0, 0)),
            pl.BlockSpec((d, 2 * width), lambda b, i: (0, 0)),
            pl.BlockSpec((1, width), lambda b, i: (0, 0)),
        ],
        out_specs=[
            pl.BlockSpec((None, tm, width), lambda b, i: (b, i, 0)),
            pl.BlockSpec((None, tm, width), lambda b, i: (b, i, 0)),
        ],
        out_shape=[jax.ShapeDtypeStruct((bsz, s, width), BF16)] * 2,
        scratch_shapes=[pltpu.VMEM((tm, d), BF16), pltpu.VMEM((tm, width), F32)],
        compiler_params=_compiler_params(2),
        name="gmlp_in",
    )(x, mod, g.reshape(1, d), w_in, g_v.reshape(1, width))


def _gmlp_out_kernel(u_ref, vn_ref, ws_ref, bs_ref, x_ref, mod_ref, wo_ref, o_ref, y_scr):
    n_groups, chunk, _ = ws_ref.shape
    gd = u_ref.shape[-1] // n_groups
    for n in range(u_ref.shape[0] // chunk):
        rows = slice(n * chunk, (n + 1) * chunk)
        for g in range(n_groups):
            cols = slice(g * gd, (g + 1) * gd)
            sv = jnp.dot(ws_ref[g], vn_ref[rows, cols], preferred_element_type=F32) + bs_ref[:, cols]
            y_scr[rows, cols] = (u_ref[rows, cols].astype(F32) * sv).astype(BF16)
    y = jnp.dot(y_scr[...], wo_ref[...], preferred_element_type=F32)
    o_ref[...] = x_ref[...] + mod_ref[2:3, :] * y


def _gmlp_out(u, vn, w_s, bs_full, x, mod, w_out, tm=512):
    bsz, s, d = x.shape
    width = u.shape[-1]
    n_groups, chunk, _ = w_s.shape
    return pl.pallas_call(
        _gmlp_out_kernel,
        grid=(bsz, s // tm),
        in_specs=[
            pl.BlockSpec((None, tm, width), lambda b, i: (b, i, 0)),
            pl.BlockSpec((None, tm, width), lambda b, i: (b, i, 0)),
            pl.BlockSpec((n_groups, chunk, chunk), lambda b, i: (0, 0, 0)),
            pl.BlockSpec((chunk, width), lambda b, i: (0, 0)),
            pl.BlockSpec((None, tm, d), lambda b, i: (b, i, 0)),
            pl.BlockSpec((None, 6, d), lambda b, i: (b, 0, 0)),
            pl.BlockSpec((width, d), lambda b, i: (0, 0)),
        ],
        out_specs=pl.BlockSpec((None, tm, d), lambda b, i: (b, i, 0)),
        out_shape=jax.ShapeDtypeStruct((bsz, s, d), F32),
        scratch_shapes=[pltpu.VMEM((tm, width), BF16)],
        compiler_params=_compiler_params(2),
        name="gmlp_out",
    )(u, vn, w_s, bs_full, x, mod, w_out)


def _rope(t, cos, sin_signed, half_idx):
    quarter = t.shape[-1] // 4
    partner = jnp.where(half_idx, pltpu.roll(t, t.shape[-1] - quarter, 1), pltpu.roll(t, quarter, 1))
    return t * cos + partner * sin_signed


def _qkv_kernel(x_ref, mod_ref, g_ref, wqk_ref, wvt_ref, gq_ref, gk_ref, cos_ref, sin_ref,
                q_ref, k_ref, vt_ref, h_scr, *, q_scale):
    n_kv, group, tm, hd = q_ref.shape
    h_scr[...] = _modnorm(x_ref[...], g_ref[...], mod_ref[1:2, :], mod_ref[0:1, :]).astype(BF16)
    cos = cos_ref[...]
    sin = sin_ref[...]
    lane = lax.broadcasted_iota(jnp.int32, (tm, hd), 1)
    half_idx = (lane % (hd // 2)) < (hd // 4)

    def head(t, gain, post):
        t = t * lax.rsqrt(jnp.mean(t * t, axis=-1, keepdims=True) + EPS) * gain
        return (_rope(t, cos, sin, half_idx) * post).astype(BF16)

    for kv in range(n_kv):
        cols = slice(kv * group * hd, (kv + 1) * group * hd)
        tq = jnp.dot(h_scr[...], wqk_ref[:, cols], preferred_element_type=F32)
        for g in range(group):
            q_ref[kv, g] = head(tq[:, g * hd:(g + 1) * hd], gq_ref[...], q_scale)
    k_off = n_kv * group * hd
    tk = jnp.dot(h_scr[...], wqk_ref[:, k_off:k_off + n_kv * hd], preferred_element_type=F32)
    for kv in range(n_kv):
        k_ref[kv] = head(tk[:, kv * hd:(kv + 1) * hd], gk_ref[...], 1.0)
    vt = lax.dot_general(wvt_ref[...], h_scr[...], (((1,), (1,)), ((), ())),
                         preferred_element_type=F32)
    for kv in range(n_kv):
        vt_ref[kv, 0] = vt[kv * hd:(kv + 1) * hd, :].astype(BF16)


def _qkv(x, mod, g, w_qk, w_vt, g_q, g_k, cos, sin, n_kv, tm=512):
    bsz, s, d = x.shape
    hd = g_q.shape[-1]
    n_heads = d // hd
    group = n_heads // n_kv
    kernel = functools.partial(_qkv_kernel, q_scale=math.log2(math.e) / math.sqrt(hd))
    return pl.pallas_call(
        kernel,
        grid=(bsz, s // tm),
        in_specs=[
            pl.BlockSpec((None, tm, d), lambda b, i: (b, i, 0)),
            pl.BlockSpec((None, 6, d), lambda b, i: (b, 0, 0)),
            pl.BlockSpec((1, d), lambda b, i: (0, 0)),
            pl.BlockSpec(w_qk.shape, lambda b, i: (0, 0)),
            pl.BlockSpec(w_vt.shape, lambda b, i: (0, 0)),
            pl.BlockSpec((1, hd), lambda b, i: (0, 0)),
            pl.BlockSpec((1, hd), lambda b, i: (0, 0)),
            pl.BlockSpec((tm, hd), lambda b, i: (i, 0)),
            pl.BlockSpec((tm, hd), lambda b, i: (i, 0)),
        ],
        out_specs=[
            pl.BlockSpec((None, n_kv, group, tm, hd), lambda b, i: (b, 0, 0, i, 0)),
            pl.BlockSpec((None, n_kv, tm, hd), lambda b, i: (b, 0, i, 0)),
            pl.BlockSpec((None, n_kv, 1, hd, tm), lambda b, i: (b, 0, i, 0, 0)),
        ],
        out_shape=[
            jax.ShapeDtypeStruct((bsz, n_kv, group, s, hd), BF16),
            jax.ShapeDtypeStruct((bsz, n_kv, s, hd), BF16),
            jax.ShapeDtypeStruct((bsz, n_kv, s // tm, hd, tm), BF16),
        ],
        scratch_shapes=[pltpu.VMEM((tm, d), BF16)],
        compiler_params=_compiler_params(2),
        name="qkv",
    )(x, mod, g.reshape(1, d), w_qk, w_vt, g_q.reshape(1, hd), g_k.reshape(1, hd), cos, sin)


ATTN_SLOTS = 4


def _attn_kernel(q_ref, k_ref, vt_ref, o_ref, acc_scr, *slots):
    group, tq, hd = q_ref.shape
    n_kt, _, tk = vt_ref.shape
    nq = group * tq
    s_scr, p_scr = slots[:ATTN_SLOTS], slots[ATTN_SLOTS:]

    def scores(kt, s_scr):
        k_tile = k_ref[pl.ds(pl.multiple_of(kt * tk, tk), tk), :]
        s = lax.dot_general(k_tile, q_ref[...].reshape(nq, hd), (((1,), (1,)), ((), ())),
                            preferred_element_type=F32)
        s_scr[...] = s
        return jnp.max(s, axis=0, keepdims=True)

    def softmax(s_scr, p_scr, m, l, tile_max):
        m_new = jnp.maximum(m, tile_max)
        alpha = jnp.exp2(m - m_new)
        p = jnp.exp2(s_scr[...] - m_new)
        p_scr[...] = p.astype(BF16)
        return m_new, alpha * l + jnp.sum(p, axis=0, keepdims=True), alpha

    def values(kt, p_scr, alpha):
        pv = jnp.dot(vt_ref[kt], p_scr[...], preferred_element_type=F32)
        acc_scr[...] = alpha * acc_scr[...] + pv

    def tile(t, slot, carry, prefetch):
        m, l, max_t, max_t1, alpha_t2, alpha_t1 = carry
        max_t2 = scores(t + 2, s_scr[(slot + 2) % ATTN_SLOTS]) if prefetch else None
        if alpha_t2 is not None:
            values(t - 2, p_scr[(slot - 2) % ATTN_SLOTS], alpha_t2)
        m, l, alpha = softmax(s_scr[slot], p_scr[slot], m, l, max_t)
        return m, l, max_t1, max_t2, alpha_t1, alpha

    def group_of_tiles(t0, carry, n_prefetch=ATTN_SLOTS):
        for slot in range(ATTN_SLOTS):
            carry = tile(t0 + slot, slot, carry, prefetch=slot < n_prefetch)
        return carry

    n_groups = n_kt // ATTN_SLOTS
    acc_scr[...] = jnp.zeros_like(acc_scr)
    m0 = jnp.full((1, nq), -jnp.inf, F32)
    l0 = jnp.zeros((1, nq), F32)
    carry = (m0, l0, scores(0, s_scr[0]), scores(1, s_scr[1]), None, None)
    carry = group_of_tiles(0, carry)
    carry = lax.fori_loop(1, n_groups - 1, lambda g, c: group_of_tiles(g * ATTN_SLOTS, c), carry)
    _, l, _, _, alpha_t2, alpha_t1 = group_of_tiles(n_kt - ATTN_SLOTS, carry,
                                                    n_prefetch=ATTN_SLOTS - 2)
    values(n_kt - 2, p_scr[ATTN_SLOTS - 2], alpha_t2)
    values(n_kt - 1, p_scr[ATTN_SLOTS - 1], alpha_t1)
    out_t = acc_scr[...] / l
    for g in range(group):
        o_ref[:, g * hd:(g + 1) * hd] = out_t[:, g * tq:(g + 1) * tq].T.astype(BF16)


def _attention(q, k, vt, tq=128):
    bsz, n_kv, group, s, hd = q.shape
    n_kt, tk = vt.shape[2], vt.shape[4]
    assert n_kt % ATTN_SLOTS == 0 and n_kt >= 2 * ATTN_SLOTS
    nq = group * tq
    return pl.pallas_call(
        _attn_kernel,
        grid=(bsz, n_kv, s // tq),
        in_specs=[
            pl.BlockSpec((None, None, group, tq, hd), lambda b, h, i: (b, h, 0, i, 0)),
            pl.BlockSpec((None, None, s, hd), lambda b, h, i: (b, h, 0, 0)),
            pl.BlockSpec((None, None, n_kt, hd, tk), lambda b, h, i: (b, h, 0, 0, 0)),
        ],
        out_specs=pl.BlockSpec((None, tq, group * hd), lambda b, h, i: (b, i, h)),
        out_shape=jax.ShapeDtypeStruct((bsz, s, n_kv * group * hd), BF16),
        scratch_shapes=([pltpu.VMEM((hd, nq), F32)]
                        + [pltpu.VMEM((tk, nq), F32)] * ATTN_SLOTS
                        + [pltpu.VMEM((tk, nq), BF16)] * ATTN_SLOTS),
        compiler_params=_compiler_params(3),
        name="attention",
    )(q, k, vt)


def _oproj_kernel(a_ref, w_ref, x_ref, mod_ref, o_ref):
    y = jnp.dot(a_ref[...], w_ref[...], preferred_element_type=F32)
    o_ref[...] = x_ref[...] + mod_ref[2:3, :] * y


def _oproj(a, w_o, x, mod, tm=512):
    bsz, s, d = x.shape
    return pl.pallas_call(
        _oproj_kernel,
        grid=(bsz, s // tm),
        in_specs=[
            pl.BlockSpec((None, tm, a.shape[-1]), lambda b, i: (b, i, 0)),
            pl.BlockSpec(w_o.shape, lambda b, i: (0, 0)),
            pl.BlockSpec((None, tm, d), lambda b, i: (b, i, 0)),
            pl.BlockSpec((None, 6, d), lambda b, i: (b, 0, 0)),
        ],
        out_specs=pl.BlockSpec((None, tm, d), lambda b, i: (b, i, 0)),
        out_shape=jax.ShapeDtypeStruct((bsz, s, d), F32),
        compiler_params=_compiler_params(2),
        name="oproj",
    )(a, w_o, x, mod)


def _ffn_kernel(x_ref, xp_ref, xn_ref, mod_ref, g_ref, wg_ref, wv_ref, cwg_ref, cwv_ref,
                cbg_ref, cbv_ref, wd_ref, gf_ref, o_ref, h_scr, ag_scr, av_scr, *, final_norm):
    tm = x_ref.shape[0]
    halo = xp_ref.shape[0]
    i = pl.program_id(1)
    j = pl.program_id(2)

    @pl.when(j == 0)
    def _():
        gain, scale, shift = g_ref[...], mod_ref[4:5, :], mod_ref[3:4, :]
        h_scr[halo:halo + tm, :] = _modnorm(x_ref[...], gain, scale, shift).astype(BF16)
        hp = _modnorm(xp_ref[...], gain, scale, shift)
        hn = _modnorm(xn_ref[...], gain, scale, shift)
        h_scr[0:halo, :] = jnp.where(i == 0, 0.0, hp).astype(BF16)
        h_scr[halo + tm:, :] = jnp.where(i == pl.num_programs(1) - 1, 0.0, hn).astype(BF16)
        o_ref[...] = jnp.zeros_like(o_ref)

    def conv(a_scr, cw_ref, cb_ref):
        return (cw_ref[0:1, :] * a_scr[halo - 1:halo - 1 + tm, :]
                + cw_ref[1:2, :] * a_scr[halo:halo + tm, :]
                + cw_ref[2:3, :] * a_scr[halo + 1:halo + 1 + tm, :]
                + cb_ref[...])

    ag_scr[...] = jnp.dot(h_scr[...], wg_ref[...], preferred_element_type=F32)
    av_scr[...] = jnp.dot(h_scr[...], wv_ref[...], preferred_element_type=F32)
    act = (_gelu(conv(ag_scr, cwg_ref, cbg_ref)) * conv(av_scr, cwv_ref, cbv_ref)).astype(BF16)
    o_ref[...] += jnp.dot(act, wd_ref[...], preferred_element_type=F32)

    @pl.when(j == pl.num_programs(2) - 1)
    def _():
        y = x_ref[...] + mod_ref[5:6, :] * o_ref[...]
        if final_norm:
            y = (y * _rms_scale(y)) * gf_ref[...]
        o_ref[...] = y


def _ffn(x, mod, g, w_up, conv_w, conv_b, w_down, g_final, final_norm, tm=512, tf=512):
    bsz, s, d = x.shape
    dff = w_down.shape[0]
    nj = dff // tf
    halo = V7X_SUBLANES
    nhb = tm // halo
    last_hb = s // halo - 1
    kernel = functools.partial(_ffn_kernel, final_norm=final_norm)
    return pl.pallas_call(
        kernel,
        grid=(bsz, s // tm, nj),
        in_specs=[
            pl.BlockSpec((None, tm, d), lambda b, i, j: (b, i, 0)),
            pl.BlockSpec((None, halo, d), lambda b, i, j: (b, jnp.maximum(i * nhb - 1, 0), 0)),
            pl.BlockSpec((None, halo, d), lambda b, i, j: (b, jnp.minimum((i + 1) * nhb, last_hb), 0)),
            pl.BlockSpec((None, 6, d), lambda b, i, j: (b, 0, 0)),
            pl.BlockSpec((1, d), lambda b, i, j: (0, 0)),
            pl.BlockSpec((d, tf), lambda b, i, j: (0, j)),
            pl.BlockSpec((d, tf), lambda b, i, j: (0, nj + j)),
            pl.BlockSpec((3, tf), lambda b, i, j: (0, j)),
            pl.BlockSpec((3, tf), lambda b, i, j: (0, nj + j)),
            pl.BlockSpec((1, tf), lambda b, i, j: (0, j)),
            pl.BlockSpec((1, tf), lambda b, i, j: (0, nj + j)),
            pl.BlockSpec((tf, d), lambda b, i, j: (j, 0)),
            pl.BlockSpec((1, d), lambda b, i, j: (0, 0)),
        ],
        out_specs=pl.BlockSpec((None, tm, d), lambda b, i, j: (b, i, 0)),
        out_shape=jax.ShapeDtypeStruct((bsz, s, d), F32),
        scratch_shapes=[
            pltpu.VMEM((tm + 2 * halo, d), BF16),
            pltpu.VMEM((tm + 2 * halo, tf), F32),
            pltpu.VMEM((tm + 2 * halo, tf), F32),
        ],
        compiler_params=_compiler_params(3),
        name="ffn",
    )(x, x, x, mod, g.reshape(1, d), w_up, w_up, conv_w, conv_w,
      conv_b.reshape(1, -1), conv_b.reshape(1, -1), w_down, g_final.reshape(1, d))


def _rope_tables(s, hd):
    axis_dim = hd // 2
    pos = jnp.arange(s)
    inv_freq = ROPE_THETA ** (-jnp.arange(0, axis_dim, 2, dtype=F32) / axis_dim)
    ang_r = (pos // GRID_W).astype(F32)[:, None] * inv_freq[None, :]
    ang_c = (pos % GRID_W).astype(F32)[:, None] * inv_freq[None, :]
    cos = jnp.concatenate([jnp.cos(ang_r)] * 2 + [jnp.cos(ang_c)] * 2, axis=-1)
    sin = jnp.concatenate([-jnp.sin(ang_r), jnp.sin(ang_r), -jnp.sin(ang_c), jnp.sin(ang_c)], axis=-1)
    return cos, sin


def kernel(x, c, w_ada, b_ada, g_norm, g_final, a_w_in, a_g_v, a_w_s, a_b_s, a_w_out,
           b_w_qkv, b_g_q, b_g_k, b_w_o, f_w_up, f_conv_w, f_conv_b, f_w_down):
    depth = w_ada.shape[0]
    bsz, s, d = x.shape
    hd = b_g_q.shape[-1]
    n_heads = d // hd
    n_kv = (b_w_qkv.shape[-1] // hd - n_heads) // 2
    n_mixers = 2

    mod = _adaln(c, w_ada, b_ada)
    cos, sin = _rope_tables(s, hd)

    for i in range(depth):
        j = i // n_mixers
        if i % n_mixers == 0:
            n_groups, chunk, _ = a_w_s.shape[1:]
            width = a_w_out.shape[1]
            bs_full = jnp.repeat(jnp.transpose(a_b_s[j]), width // n_groups, axis=1)
            u, vn = _gmlp_in(x, mod[i], g_norm[i, 0], a_w_in[j].astype(BF16), a_g_v[j])
            x = _gmlp_out(u, vn, a_w_s[j].astype(BF16), bs_full, x, mod[i], a_w_out[j].astype(BF16))
        else:
            w = b_w_qkv[j]
            qk_cols = (n_heads + n_kv) * hd
            q, k, vt = _qkv(x, mod[i], g_norm[i, 0], w[:, :qk_cols].astype(BF16),
                            jnp.transpose(w[:, qk_cols:]).astype(BF16), b_g_q[j], b_g_k[j],
                            cos, sin, n_kv)
            a = _attention(q, k, vt)
            x = _oproj(a, b_w_o[j].astype(BF16), x, mod[i])
        x = _ffn(x, mod[i], g_norm[i, 1], f_w_up[i].astype(BF16), f_conv_w[i], f_conv_b[i],
                 f_w_down[i].astype(BF16), g_final, final_norm=(i == depth - 1))
    return x
```

```python
import functools
import math

import jax
import jax.numpy as jnp
from jax import lax
from jax.experimental import pallas as pl
from jax.experimental.pallas import tpu as pltpu

EPS = 1e-6
GRID_W = 64
ROPE_THETA = 10000.0
V7X_VMEM_BYTES = 64 * 1024 * 1024
V7X_SUBLANES = 8
V7X_BF16_SUBLANES = 16
VMEM_LIMIT_BYTES = V7X_VMEM_BYTES - 8 * 1024 * 1024

BF16 = jnp.bfloat16
F32 = jnp.float32


def _compiler_params(n_grid_axes):
    return pltpu.CompilerParams(
        dimension_semantics=("arbitrary",) * n_grid_axes,
        vmem_limit_bytes=VMEM_LIMIT_BYTES,
    )


def _rms_scale(x):
    return lax.rsqrt(jnp.mean(x * x, axis=-1, keepdims=True) + EPS)


def _modnorm(x, gain, scale, shift):
    return (x * _rms_scale(x)) * (gain * (1.0 + scale)) + shift


def _gelu(x):
    return 0.5 * x * (1.0 + lax.erf(x * (1.0 / math.sqrt(2.0))))


def _adaln_kernel(c_ref, w_ref, b_ref, o_ref):
    c = c_ref[...]
    cond = c * (1.0 / (1.0 + jnp.exp(-c)))
    o_ref[...] = jnp.dot(cond, w_ref[...], precision=lax.Precision.HIGHEST,
                         preferred_element_type=F32) + b_ref[...]


def _adaln(c, w_ada, b_ada, tn=1024):
    depth, d, n = w_ada.shape
    bsz = c.shape[0]
    rows = -(-bsz // V7X_SUBLANES) * V7X_SUBLANES
    c_pad = jnp.zeros((rows, d), F32).at[:bsz].set(c)
    out = pl.pallas_call(
        _adaln_kernel,
        grid=(depth, n // tn),
        in_specs=[
            pl.BlockSpec((rows, d), lambda l, j: (0, 0)),
            pl.BlockSpec((None, d, tn), lambda l, j: (l, 0, j)),
            pl.BlockSpec((None, 1, tn), lambda l, j: (l, 0, j)),
        ],
        out_specs=pl.BlockSpec((None, rows, tn), lambda l, j: (l, 0, j)),
        out_shape=jax.ShapeDtypeStruct((depth, rows, n), F32),
        compiler_params=_compiler_params(2),
        name="adaln",
    )(c_pad, w_ada, b_ada.reshape(depth, 1, n))
    return out[:, :bsz].reshape(depth, bsz, 6, d)


def _gmlp_in_kernel(x_ref, mod_ref, g_ref, w_ref, gv_ref, u_ref, vn_ref, h_scr, v_scr, *, nc):
    width = u_ref.shape[-1]
    h_scr[...] = _modnorm(x_ref[...], g_ref[...], mod_ref[1:2, :], mod_ref[0:1, :]).astype(BF16)
    ssq = jnp.zeros((x_ref.shape[0], 1), F32)
    for c in range(width // nc):
        cols = slice(c * nc, (c + 1) * nc)
        zu = _gelu(jnp.dot(h_scr[...], w_ref[:, cols], preferred_element_type=F32))
        u_ref[:, cols] = zu.astype(BF16)
        zv = _gelu(jnp.dot(h_scr[...], w_ref[:, width + c * nc: width + (c + 1) * nc],
                           preferred_element_type=F32))
        v_scr[:, cols] = zv
        ssq = ssq + jnp.sum(zv * zv, axis=-1, keepdims=True)
    r = lax.rsqrt(ssq * (1.0 / width) + EPS)
    vn_ref[...] = ((v_scr[...] * r) * gv_ref[...]).astype(BF16)


def _gmlp_in(x, mod, g, w_in, g_v, tm=512, nc=512):
    bsz, s, d = x.shape
    width = w_in.shape[1] // 2
    kernel = functools.partial(_gmlp_in_kernel, nc=nc)
    return pl.pallas_call(
        kernel,
        grid=(bsz, s // tm),
        in_specs=[
            pl.BlockSpec((None, tm, d), lambda b, i: (b, i, 0)),
            pl.BlockSpec((None, 6, d), lambda b, i: (b, 0, 0)),
            pl.BlockSpec((1, d), lambda b, i: (0, 0)),
            pl.BlockSpec((d, 2 * width), lambda b, i: (0, 0)),
            pl.BlockSpec((1, width), lambda b, i: (0, 0)),
        ],
        out_specs=[
            pl.BlockSpec((None, tm, width), lambda b, i: (b, i, 0)),
            pl.BlockSpec((None, tm, width), lambda b, i: (b, i, 0)),
        ],
        out_shape=[jax.ShapeDtypeStruct((bsz, s, width), BF16)] * 2,
        scratch_shapes=[pltpu.VMEM((tm, d), BF16), pltpu.VMEM((tm, width), F32)],
        compiler_params=_compiler_params(2),
        name="gmlp_in",
    )(x, mod, g.reshape(1, d), w_in, g_v.reshape(1, width))


def _gmlp_out_kernel(u_ref, vn_ref, ws_ref, bs_ref, x_ref, mod_ref, wo_ref, o_ref, y_scr):
    n_groups, chunk, _ = ws_ref.shape
    gd = u_ref.shape[-1] // n_groups
    for n in range(u_ref.shape[0] // chunk):
        rows = slice(n * chunk, (n + 1) * chunk)
        for g in range(n_groups):
            cols = slice(g * gd, (g + 1) * gd)
            sv = jnp.dot(ws_ref[g], vn_ref[rows, cols], preferred_element_type=F32) + bs_ref[:, cols]
            y_scr[rows, cols] = (u_ref[rows, cols].astype(F32) * sv).astype(BF16)
    y = jnp.dot(y_scr[...], wo_ref[...], preferred_element_type=F32)
    o_ref[...] = x_ref[...] + mod_ref[2:3, :] * y


def _gmlp_out(u, vn, w_s, bs_full, x, mod, w_out, tm=512):
    bsz, s, d = x.shape
    width = u.shape[-1]
    n_groups, chunk, _ = w_s.shape
    return pl.pallas_call(
        _gmlp_out_kernel,
        grid=(bsz, s // tm),
        in_specs=[
            pl.BlockSpec((None, tm, width), lambda b, i: (b, i, 0)),
            pl.BlockSpec((None, tm, width), lambda b, i: (b, i, 0)),
            pl.BlockSpec((n_groups, chunk, chunk), lambda b, i: (0, 0, 0)),
            pl.BlockSpec((chunk, width), lambda b, i: (0, 0)),
            pl.BlockSpec((None, tm, d), lambda b, i: (b, i, 0)),
            pl.BlockSpec((None, 6, d), lambda b, i: (b, 0, 0)),
            pl.BlockSpec((width, d), lambda b, i: (0, 0)),
        ],
        out_specs=pl.BlockSpec((None, tm, d), lambda b, i: (b, i, 0)),
        out_shape=jax.ShapeDtypeStruct((bsz, s, d), F32),
        scratch_shapes=[pltpu.VMEM((tm, width), BF16)],
        compiler_params=_compiler_params(2),
        name="gmlp_out",
    )(u, vn, w_s, bs_full, x, mod, w_out)


def _rope(t, cos, sin_signed, half_idx):
    quarter = t.shape[-1] // 4
    partner = jnp.where(half_idx, pltpu.roll(t, t.shape[-1] - quarter, 1), pltpu.roll(t, quarter, 1))
    return t * cos + partner * sin_signed


def _qkv_kernel(x_ref, mod_ref, g_ref, wqk_ref, wvt_ref, gq_ref, gk_ref, cos_ref, sin_ref,
                q_ref, k_ref, vt_ref, h_scr, *, q_scale):
    n_kv, group, tm, hd = q_ref.shape
    h_scr[...] = _modnorm(x_ref[...], g_ref[...], mod_ref[1:2, :], mod_ref[0:1, :]).astype(BF16)
    cos = cos_ref[...]
    sin = sin_ref[...]
    lane = lax.broadcasted_iota(jnp.int32, (tm, hd), 1)
    half_idx = (lane % (hd // 2)) < (hd // 4)

    def head(t, gain, post):
        t = t * lax.rsqrt(jnp.mean(t * t, axis=-1, keepdims=True) + EPS) * gain
        return (_rope(t, cos, sin, half_idx) * post).astype(BF16)

    for kv in range(n_kv):
        cols = slice(kv * group * hd, (kv + 1) * group * hd)
        tq = jnp.dot(h_scr[...], wqk_ref[:, cols], preferred_element_type=F32)
        for g in range(group):
            q_ref[kv, g] = head(tq[:, g * hd:(g + 1) * hd], gq_ref[...], q_scale)
    k_off = n_kv * group * hd
    tk = jnp.dot(h_scr[...], wqk_ref[:, k_off:k_off + n_kv * hd], preferred_element_type=F32)
    for kv in range(n_kv):
        k_ref[kv] = head(tk[:, kv * hd:(kv + 1) * hd], gk_ref[...], 1.0)
    vt = lax.dot_general(wvt_ref[...], h_scr[...], (((1,), (1,)), ((), ())),
                         preferred_element_type=F32)
    pad_rows = vt_ref.shape[2] - hd
    ones_row = lax.broadcasted_iota(jnp.int32, (pad_rows, tm), 0) == 0
    for kv in range(n_kv):
        vt_ref[kv, 0, :hd, :] = vt[kv * hd:(kv + 1) * hd, :].astype(BF16)
        vt_ref[kv, 0, hd:, :] = ones_row.astype(BF16)


def _qkv(x, mod, g, w_qk, w_vt, g_q, g_k, cos, sin, n_kv, tm=512):
    bsz, s, d = x.shape
    hd = g_q.shape[-1]
    n_heads = d // hd
    group = n_heads // n_kv
    kernel = functools.partial(_qkv_kernel, q_scale=math.log2(math.e) / math.sqrt(hd))
    return pl.pallas_call(
        kernel,
        grid=(bsz, s // tm),
        in_specs=[
            pl.BlockSpec((None, tm, d), lambda b, i: (b, i, 0)),
            pl.BlockSpec((None, 6, d), lambda b, i: (b, 0, 0)),
            pl.BlockSpec((1, d), lambda b, i: (0, 0)),
            pl.BlockSpec(w_qk.shape, lambda b, i: (0, 0)),
            pl.BlockSpec(w_vt.shape, lambda b, i: (0, 0)),
            pl.BlockSpec((1, hd), lambda b, i: (0, 0)),
            pl.BlockSpec((1, hd), lambda b, i: (0, 0)),
            pl.BlockSpec((tm, hd), lambda b, i: (i, 0)),
            pl.BlockSpec((tm, hd), lambda b, i: (i, 0)),
        ],
        out_specs=[
            pl.BlockSpec((None, n_kv, group, tm, hd), lambda b, i: (b, 0, 0, i, 0)),
            pl.BlockSpec((None, n_kv, tm, hd), lambda b, i: (b, 0, i, 0)),
            pl.BlockSpec((None, n_kv, 1, hd + V7X_BF16_SUBLANES, tm), lambda b, i: (b, 0, i, 0, 0)),
        ],
        out_shape=[
            jax.ShapeDtypeStruct((bsz, n_kv, group, s, hd), BF16),
            jax.ShapeDtypeStruct((bsz, n_kv, s, hd), BF16),
            jax.ShapeDtypeStruct((bsz, n_kv, s // tm, hd + V7X_BF16_SUBLANES, tm), BF16),
        ],
        scratch_shapes=[pltpu.VMEM((tm, d), BF16)],
        compiler_params=_compiler_params(2),
        name="qkv",
    )(x, mod, g.reshape(1, d), w_qk, w_vt, g_q.reshape(1, hd), g_k.reshape(1, hd), cos, sin)


ATTN_SLOTS = 4


def _attn_kernel(q_ref, k_ref, vt_ref, o_ref, acc_scr, *slots):
    group, tq, hd = q_ref.shape
    n_kt, _, tk = vt_ref.shape
    nq = group * tq
    s_scr, p_scr = slots[:ATTN_SLOTS], slots[ATTN_SLOTS:]
    st_idx = jnp.minimum(pl.program_id(2), 0)
    ld_idx = jnp.minimum(pl.program_id(1), 0)

    def scores(kt, s_scr):
        k_tile = k_ref[pl.ds(pl.multiple_of(kt * tk, tk), tk), :]
        s = lax.dot_general(k_tile, q_ref[...].reshape(nq, hd), (((1,), (1,)), ((), ())),
                            preferred_element_type=F32)
        s_scr[st_idx] = s
        return jnp.max(s, axis=0, keepdims=True)

    def softmax(s_scr, p_scr, m, tile_max):
        m_new = jnp.maximum(m, tile_max)
        p_scr[st_idx] = jnp.exp2(s_scr[ld_idx] - m_new).astype(BF16)
        return m_new, jnp.exp2(m - m_new)

    def values(kt, p_scr, alpha):
        pv = jnp.dot(vt_ref[kt], p_scr[ld_idx], preferred_element_type=F32)
        acc_scr[...] = alpha * acc_scr[...] + pv

    def tile(t, slot, carry, prefetch):
        m, max_t, max_t1, alpha_t2, alpha_t1 = carry
        max_t2 = scores(t + 2, s_scr[(slot + 2) % ATTN_SLOTS]) if prefetch else None
        if alpha_t2 is not None:
            values(t - 2, p_scr[(slot - 2) % ATTN_SLOTS], alpha_t2)
        m, alpha = softmax(s_scr[slot], p_scr[slot], m, max_t)
        return m, max_t1, max_t2, alpha_t1, alpha

    def group_of_tiles(t0, carry, n_prefetch=ATTN_SLOTS):
        for slot in range(ATTN_SLOTS):
            carry = tile(t0 + slot, slot, carry, prefetch=slot < n_prefetch)
        return carry

    n_groups = n_kt // ATTN_SLOTS
    acc_scr[...] = jnp.zeros_like(acc_scr)
    m0 = jnp.full((1, nq), -jnp.inf, F32)
    carry = (m0, scores(0, s_scr[0]), scores(1, s_scr[1]), None, None)
    carry = group_of_tiles(0, carry)
    carry = lax.fori_loop(1, n_groups - 1, lambda g, c: group_of_tiles(g * ATTN_SLOTS, c), carry,
                          unroll=True)
    _, _, _, alpha_t2, alpha_t1 = group_of_tiles(n_kt - ATTN_SLOTS, carry,
                                                 n_prefetch=ATTN_SLOTS - 2)
    values(n_kt - 2, p_scr[ATTN_SLOTS - 2], alpha_t2)
    values(n_kt - 1, p_scr[ATTN_SLOTS - 1], alpha_t1)
    out_t = acc_scr[:hd, :] / acc_scr[hd:hd + 1, :]
    for g in range(group):
        o_ref[:, g * hd:(g + 1) * hd] = out_t[:, g * tq:(g + 1) * tq].T.astype(BF16)


def _attention(q, k, vt, tq=128):
    bsz, n_kv, group, s, hd = q.shape
    n_kt, tk = vt.shape[2], vt.shape[4]
    assert n_kt % ATTN_SLOTS == 0 and n_kt >= 2 * ATTN_SLOTS
    nq = group * tq
    return pl.pallas_call(
        _attn_kernel,
        grid=(bsz, n_kv, s // tq),
        in_specs=[
            pl.BlockSpec((None, None, group, tq, hd), lambda b, h, i: (b, h, 0, i, 0)),
            pl.BlockSpec((None, None, s, hd), lambda b, h, i: (b, h, 0, 0)),
            pl.BlockSpec((None, None, n_kt, vt.shape[3], tk), lambda b, h, i: (b, h, 0, 0, 0)),
        ],
        out_specs=pl.BlockSpec((None, tq, group * hd), lambda b, h, i: (b, i, h)),
        out_shape=jax.ShapeDtypeStruct((bsz, s, n_kv * group * hd), BF16),
        scratch_shapes=([pltpu.VMEM((vt.shape[3], nq), F32)]
                        + [pltpu.VMEM((1, tk, nq), F32)] * ATTN_SLOTS
                        + [pltpu.VMEM((1, tk, nq), BF16)] * ATTN_SLOTS),
        compiler_params=_compiler_params(3),
        name="attention",
    )(q, k, vt)


def _oproj_kernel(a_ref, w_ref, x_ref, mod_ref, o_ref):
    y = jnp.dot(a_ref[...], w_ref[...], preferred_element_type=F32)
    o_ref[...] = x_ref[...] + mod_ref[2:3, :] * y


def _oproj(a, w_o, x, mod, tm=512):
    bsz, s, d = x.shape
    return pl.pallas_call(
        _oproj_kernel,
        grid=(bsz, s // tm),
        in_specs=[
            pl.BlockSpec((None, tm, a.shape[-1]), lambda b, i: (b, i, 0)),
            pl.BlockSpec(w_o.shape, lambda b, i: (0, 0)),
            pl.BlockSpec((None, tm, d), lambda b, i: (b, i, 0)),
            pl.BlockSpec((None, 6, d), lambda b, i: (b, 0, 0)),
        ],
        out_specs=pl.BlockSpec((None, tm, d), lambda b, i: (b, i, 0)),
        out_shape=jax.ShapeDtypeStruct((bsz, s, d), F32),
        compiler_params=_compiler_params(2),
        name="oproj",
    )(a, w_o, x, mod)


def _ffn_kernel(x_ref, xp_ref, xn_ref, mod_ref, g_ref, wg_ref, wv_ref, cwg_ref, cwv_ref,
                cbg_ref, cbv_ref, wd_ref, gf_ref, o_ref, h_scr, ag_scr, av_scr, *, final_norm):
    tm = x_ref.shape[0]
    halo = xp_ref.shape[0]
    i = pl.program_id(1)
    j = pl.program_id(2)

    @pl.when(j == 0)
    def _():
        gain, scale, shift = g_ref[...], mod_ref[4:5, :], mod_ref[3:4, :]
        h_scr[halo:halo + tm, :] = _modnorm(x_ref[...], gain, scale, shift).astype(BF16)
        hp = _modnorm(xp_ref[...], gain, scale, shift)
        hn = _modnorm(xn_ref[...], gain, scale, shift)
        h_scr[0:halo, :] = jnp.where(i == 0, 0.0, hp).astype(BF16)
        h_scr[halo + tm:, :] = jnp.where(i == pl.num_programs(1) - 1, 0.0, hn).astype(BF16)
        o_ref[...] = jnp.zeros_like(o_ref)

    def conv(a_scr, cw_ref, cb_ref):
        return (cw_ref[0:1, :] * a_scr[halo - 1:halo - 1 + tm, :]
                + cw_ref[1:2, :] * a_scr[halo:halo + tm, :]
                + cw_ref[2:3, :] * a_scr[halo + 1:halo + 1 + tm, :]
                + cb_ref[...])

    ag_scr[...] = jnp.dot(h_scr[...], wg_ref[...], preferred_element_type=F32)
    av_scr[...] = jnp.dot(h_scr[...], wv_ref[...], preferred_element_type=F32)
    act = (_gelu(conv(ag_scr, cwg_ref, cbg_ref)) * conv(av_scr, cwv_ref, cbv_ref)).astype(BF16)
    o_ref[...] += jnp.dot(act, wd_ref[...], preferred_element_type=F32)

    @pl.when(j == pl.num_programs(2) - 1)
    def _():
        y = x_ref[...] + mod_ref[5:6, :] * o_ref[...]
        if final_norm:
            y = (y * _rms_scale(y)) * gf_ref[...]
        o_ref[...] = y


def _ffn(x, mod, g, w_up, conv_w, conv_b, w_down, g_final, final_norm, tm=512, tf=512):
    bsz, s, d = x.shape
    dff = w_down.shape[0]
    nj = dff // tf
    halo = V7X_SUBLANES
    nhb = tm // halo
    last_hb = s // halo - 1
    kernel = functools.partial(_ffn_kernel, final_norm=final_norm)
    return pl.pallas_call(
        kernel,
        grid=(bsz, s // tm, nj),
        in_specs=[
            pl.BlockSpec((None, tm, d), lambda b, i, j: (b, i, 0)),
            pl.BlockSpec((None, halo, d), lambda b, i, j: (b, jnp.maximum(i * nhb - 1, 0), 0)),
            pl.BlockSpec((None, halo, d), lambda b, i, j: (b, jnp.minimum((i + 1) * nhb, last_hb), 0)),
            pl.BlockSpec((None, 6, d), lambda b, i, j: (b, 0, 0)),
            pl.BlockSpec((1, d), lambda b, i, j: (0, 0)),
            pl.BlockSpec((d, tf), lambda b, i, j: (0, j)),
            pl.BlockSpec((d, tf), lambda b, i, j: (0, nj + j)),
            pl.BlockSpec((3, tf), lambda b, i, j: (0, j)),
            pl.BlockSpec((3, tf), lambda b, i, j: (0, nj + j)),
            pl.BlockSpec((1, tf), lambda b, i, j: (0, j)),
            pl.BlockSpec((1, tf), lambda b, i, j: (0, nj + j)),
            pl.BlockSpec((tf, d), lambda b, i, j: (j, 0)),
            pl.BlockSpec((1, d), lambda b, i, j: (0, 0)),
        ],
        out_specs=pl.BlockSpec((None, tm, d), lambda b, i, j: (b, i, 0)),
        out_shape=jax.ShapeDtypeStruct((bsz, s, d), F32),
        scratch_shapes=[
            pltpu.VMEM((tm + 2 * halo, d), BF16),
            pltpu.VMEM((tm + 2 * halo, tf), F32),
            pltpu.VMEM((tm + 2 * halo, tf), F32),
        ],
        compiler_params=_compiler_params(3),
        name="ffn",
    )(x, x, x, mod, g.reshape(1, d), w_up, w_up, conv_w, conv_w,
      conv_b.reshape(1, -1), conv_b.reshape(1, -1), w_down, g_final.reshape(1, d))


def _rope_tables(s, hd):
    axis_dim = hd // 2
    pos = jnp.arange(s)
    inv_freq = ROPE_THETA ** (-jnp.arange(0, axis_dim, 2, dtype=F32) / axis_dim)
    ang_r = (pos // GRID_W).astype(F32)[:, None] * inv_freq[None, :]
    ang_c = (pos % GRID_W).astype(F32)[:, None] * inv_freq[None, :]
    cos = jnp.concatenate([jnp.cos(ang_r)] * 2 + [jnp.cos(ang_c)] * 2, axis=-1)
    sin = jnp.concatenate([-jnp.sin(ang_r), jnp.sin(ang_r), -jnp.sin(ang_c), jnp.sin(ang_c)], axis=-1)
    return cos, sin


def kernel(x, c, w_ada, b_ada, g_norm, g_final, a_w_in, a_g_v, a_w_s, a_b_s, a_w_out,
           b_w_qkv, b_g_q, b_g_k, b_w_o, f_w_up, f_conv_w, f_conv_b, f_w_down):
    depth = w_ada.shape[0]
    bsz, s, d = x.shape
    hd = b_g_q.shape[-1]
    n_heads = d // hd
    n_kv = (b_w_qkv.shape[-1] // hd - n_heads) // 2
    n_mixers = 2

    mod = _adaln(c, w_ada, b_ada)
    cos, sin = _rope_tables(s, hd)

    for i in range(depth):
        j = i // n_mixers
        if i % n_mixers == 0:
            n_groups, chunk, _ = a_w_s.shape[1:]
            width = a_w_out.shape[1]
            bs_full = jnp.repeat(jnp.transpose(a_b_s[j]), width // n_groups, axis=1)
            u, vn = _gmlp_in(x, mod[i], g_norm[i, 0], a_w_in[j].astype(BF16), a_g_v[j])
            x = _gmlp_out(u, vn, a_w_s[j].astype(BF16), bs_full, x, mod[i], a_w_out[j].astype(BF16))
        else:
            w = b_w_qkv[j]
            qk_cols = (n_heads + n_kv) * hd
            q, k, vt = _qkv(x, mod[i], g_norm[i, 0], w[:, :qk_cols].astype(BF16),
                            jnp.transpose(w[:, qk_cols:]).astype(BF16), b_g_q[j], b_g_k[j],
                            cos, sin, n_kv)
            a = _attention(q, k, vt)
            x = _oproj(a, b_w_o[j].astype(BF16), x, mod[i])
        x = _ffn(x, mod[i], g_norm[i, 1], f_w_up[i].astype(BF16), f_conv_w[i], f_conv_b[i],
                 f_w_down[i].astype(BF16), g_final, final_norm=(i == depth - 1))
    return x
```

```python
import functools
import math

import jax
import jax.numpy as jnp
from jax import lax
from jax.experimental import pallas as pl
from jax.experimental.pallas import tpu as pltpu

EPS = 1e-6
GRID_W = 64
ROPE_THETA = 10000.0
V7X_VMEM_BYTES = 64 * 1024 * 1024
V7X_SUBLANES = 8
V7X_BF16_SUBLANES = 16
VMEM_LIMIT_BYTES = V7X_VMEM_BYTES - 8 * 1024 * 1024

BF16 = jnp.bfloat16
F32 = jnp.float32


def _compiler_params(n_grid_axes):
    return pltpu.CompilerParams(
        dimension_semantics=("arbitrary",) * n_grid_axes,
        vmem_limit_bytes=VMEM_LIMIT_BYTES,
    )


def _rms_scale(x):
    return lax.rsqrt(jnp.mean(x * x, axis=-1, keepdims=True) + EPS)


def _modnorm(x, gain, scale, shift):
    return (x * _rms_scale(x)) * (gain * (1.0 + scale)) + shift


def _gelu(x):
    return 0.5 * x * (1.0 + lax.erf(x * (1.0 / math.sqrt(2.0))))


def _adaln_kernel(c_ref, w_ref, b_ref, o_ref):
    c = c_ref[...]
    cond = c * (1.0 / (1.0 + jnp.exp(-c)))
    o_ref[...] = jnp.dot(cond, w_ref[...], precision=lax.Precision.HIGHEST,
                         preferred_element_type=F32) + b_ref[...]


def _adaln(c, w_ada, b_ada, tn=1024):
    depth, d, n = w_ada.shape
    bsz = c.shape[0]
    rows = -(-bsz // V7X_SUBLANES) * V7X_SUBLANES
    c_pad = jnp.zeros((rows, d), F32).at[:bsz].set(c)
    out = pl.pallas_call(
        _adaln_kernel,
        grid=(depth, n // tn),
        in_specs=[
            pl.BlockSpec((rows, d), lambda l, j: (0, 0)),
            pl.BlockSpec((None, d, tn), lambda l, j: (l, 0, j)),
            pl.BlockSpec((None, 1, tn), lambda l, j: (l, 0, j)),
        ],
        out_specs=pl.BlockSpec((None, rows, tn), lambda l, j: (l, 0, j)),
        out_shape=jax.ShapeDtypeStruct((depth, rows, n), F32),
        compiler_params=_compiler_params(2),
        name="adaln",
    )(c_pad, w_ada, b_ada.reshape(depth, 1, n))
    return out[:, :bsz].reshape(depth, bsz, 6, d)


def _gmlp_in_kernel(x_ref, mod_ref, g_ref, w_ref, gv_ref, u_ref, vn_ref, h_scr, v_scr, *, nc):
    width = u_ref.shape[-1]
    h_scr[...] = _modnorm(x_ref[...], g_ref[...], mod_ref[1:2, :], mod_ref[0:1, :]).astype(BF16)
    ssq = jnp.zeros((x_ref.shape[0], 1), F32)
    for c in range(width // nc):
        cols = slice(c * nc, (c + 1) * nc)
        zu = _gelu(jnp.dot(h_scr[...], w_ref[:, cols], preferred_element_type=F32))
        u_ref[:, cols] = zu.astype(BF16)
        zv = _gelu(jnp.dot(h_scr[...], w_ref[:, width + c * nc: width + (c + 1) * nc],
                           preferred_element_type=F32))
        v_scr[:, cols] = zv
        ssq = ssq + jnp.sum(zv * zv, axis=-1, keepdims=True)
    r = lax.rsqrt(ssq * (1.0 / width) + EPS)
    vn_ref[...] = ((v_scr[...] * r) * gv_ref[...]).astype(BF16)


def _gmlp_in(x, mod, g, w_in, g_v, tm=512, nc=512):
    bsz, s, d = x.shape
    width = w_in.shape[1] // 2
    kernel = functools.partial(_gmlp_in_kernel, nc=nc)
    return pl.pallas_call(
        kernel,
        grid=(bsz, s // tm),
        in_specs=[
            pl.BlockSpec((None, tm, d), lambda b, i: (b, i, 0)),
            pl.BlockSpec((None, 6, d), lambda b, i: (b, 0, 0)),
            pl.BlockSpec((1, d), lambda b, i: (0, 0)),
            pl.BlockSpec((d, 2 * width), lambda b, i: (0, 0)),
            pl.BlockSpec((1, width), lambda b, i: (0, 0)),
        ],
        out_specs=[
            pl.BlockSpec((None, tm, width), lambda b, i: (b, i, 0)),
            pl.BlockSpec((None, tm, width), lambda b, i: (b, i, 0)),
        ],
        out_shape=[jax.ShapeDtypeStruct((bsz, s, width), BF16)] * 2,
        scratch_shapes=[pltpu.VMEM((tm, d), BF16), pltpu.VMEM((tm, width), F32)],
        compiler_params=_compiler_params(2),
        name="gmlp_in",
    )(x, mod, g.reshape(1, d), w_in, g_v.reshape(1, width))


def _gmlp_out_kernel(u_ref, vn_ref, ws_ref, bs_ref, x_ref, mod_ref, wo_ref, o_ref, y_scr):
    n_groups, chunk, _ = ws_ref.shape
    gd = u_ref.shape[-1] // n_groups
    for n in range(u_ref.shape[0] // chunk):
        rows = slice(n * chunk, (n + 1) * chunk)
        for g in range(n_groups):
            cols = slice(g * gd, (g + 1) * gd)
            sv = jnp.dot(ws_ref[g], vn_ref[rows, cols], preferred_element_type=F32) + bs_ref[:, cols]
            y_scr[rows, cols] = (u_ref[rows, cols].astype(F32) * sv).astype(BF16)
    y = jnp.dot(y_scr[...], wo_ref[...], preferred_element_type=F32)
    o_ref[...] = x_ref[...] + mod_ref[2:3, :] * y


def _gmlp_out(u, vn, w_s, bs_full, x, mod, w_out, tm=512):
    bsz, s, d = x.shape
    width = u.shape[-1]
    n_groups, chunk, _ = w_s.shape
    return pl.pallas_call(
        _gmlp_out_kernel,
        grid=(bsz, s // tm),
        in_specs=[
            pl.BlockSpec((None, tm, width), lambda b, i: (b, i, 0)),
            pl.BlockSpec((None, tm, width), lambda b, i: (b, i, 0)),
            pl.BlockSpec((n_groups, chunk, chunk), lambda b, i: (0, 0, 0)),
            pl.BlockSpec((chunk, width), lambda b, i: (0, 0)),
            pl.BlockSpec((None, tm, d), lambda b, i: (b, i, 0)),
            pl.BlockSpec((None, 6, d), lambda b, i: (b, 0, 0)),
            pl.BlockSpec((width, d), lambda b, i: (0, 0)),
        ],
        out_specs=pl.BlockSpec((None, tm, d), lambda b, i: (b, i, 0)),
        out_shape=jax.ShapeDtypeStruct((bsz, s, d), F32),
        scratch_shapes=[pltpu.VMEM((tm, width), BF16)],
        compiler_params=_compiler_params(2),
        name="gmlp_out",
    )(u, vn, w_s, bs_full, x, mod, w_out)


def _rope(t, cos, sin_signed, half_idx):
    quarter = t.shape[-1] // 4
    partner = jnp.where(half_idx, pltpu.roll(t, t.shape[-1] - quarter, 1), pltpu.roll(t, quarter, 1))
    return t * cos + partner * sin_signed


def _qkv_kernel(x_ref, mod_ref, g_ref, wqk_ref, wvt_ref, gq_ref, gk_ref, cos_ref, sin_ref,
                q_ref, k_ref, vt_ref, h_scr, *, q_scale):
    n_kv, group, tm, hd = q_ref.shape
    h_scr[...] = _modnorm(x_ref[...], g_ref[...], mod_ref[1:2, :], mod_ref[0:1, :]).astype(BF16)
    cos = cos_ref[...]
    sin = sin_ref[...]
    lane = lax.broadcasted_iota(jnp.int32, (tm, hd), 1)
    half_idx = (lane % (hd // 2)) < (hd // 4)

    def head(t, gain, post):
        t = t * lax.rsqrt(jnp.mean(t * t, axis=-1, keepdims=True) + EPS) * gain
        return (_rope(t, cos, sin, half_idx) * post).astype(BF16)

    for kv in range(n_kv):
        cols = slice(kv * group * hd, (kv + 1) * group * hd)
        tq = jnp.dot(h_scr[...], wqk_ref[:, cols], preferred_element_type=F32)
        for g in range(group):
            q_ref[kv, g] = head(tq[:, g * hd:(g + 1) * hd], gq_ref[...], q_scale)
    k_off = n_kv * group * hd
    tk = jnp.dot(h_scr[...], wqk_ref[:, k_off:k_off + n_kv * hd], preferred_element_type=F32)
    for kv in range(n_kv):
        k_ref[kv] = head(tk[:, kv * hd:(kv + 1) * hd], gk_ref[...], 1.0)
    vt = lax.dot_general(wvt_ref[...], h_scr[...], (((1,), (1,)), ((), ())),
                         preferred_element_type=F32)
    pad_rows = vt_ref.shape[2] - hd
    ones_row = lax.broadcasted_iota(jnp.int32, (pad_rows, tm), 0) == 0
    for kv in range(n_kv):
        vt_ref[kv, 0, :hd, :] = vt[kv * hd:(kv + 1) * hd, :].astype(BF16)
        vt_ref[kv, 0, hd:, :] = ones_row.astype(BF16)


def _qkv(x, mod, g, w_qk, w_vt, g_q, g_k, cos, sin, n_kv, tm=512):
    bsz, s, d = x.shape
    hd = g_q.shape[-1]
    n_heads = d // hd
    group = n_heads // n_kv
    kernel = functools.partial(_qkv_kernel, q_scale=math.log2(math.e) / math.sqrt(hd))
    return pl.pallas_call(
        kernel,
        grid=(bsz, s // tm),
        in_specs=[
            pl.BlockSpec((None, tm, d), lambda b, i: (b, i, 0)),
            pl.BlockSpec((None, 6, d), lambda b, i: (b, 0, 0)),
            pl.BlockSpec((1, d), lambda b, i: (0, 0)),
            pl.BlockSpec(w_qk.shape, lambda b, i: (0, 0)),
            pl.BlockSpec(w_vt.shape, lambda b, i: (0, 0)),
            pl.BlockSpec((1, hd), lambda b, i: (0, 0)),
            pl.BlockSpec((1, hd), lambda b, i: (0, 0)),
            pl.BlockSpec((tm, hd), lambda b, i: (i, 0)),
            pl.BlockSpec((tm, hd), lambda b, i: (i, 0)),
        ],
        out_specs=[
            pl.BlockSpec((None, n_kv, group, tm, hd), lambda b, i: (b, 0, 0, i, 0)),
            pl.BlockSpec((None, n_kv, tm, hd), lambda b, i: (b, 0, i, 0)),
            pl.BlockSpec((None, n_kv, 1, hd + V7X_BF16_SUBLANES, tm), lambda b, i: (b, 0, i, 0, 0)),
        ],
        out_shape=[
            jax.ShapeDtypeStruct((bsz, n_kv, group, s, hd), BF16),
            jax.ShapeDtypeStruct((bsz, n_kv, s, hd), BF16),
            jax.ShapeDtypeStruct((bsz, n_kv, s // tm, hd + V7X_BF16_SUBLANES, tm), BF16),
        ],
        scratch_shapes=[pltpu.VMEM((tm, d), BF16)],
        compiler_params=_compiler_params(2),
        name="qkv",
    )(x, mod, g.reshape(1, d), w_qk, w_vt, g_q.reshape(1, hd), g_k.reshape(1, hd), cos, sin)


ATTN_SLOTS = 4


def _attn_kernel(q_ref, k_ref, vt_ref, o_ref, acc_scr, *slots):
    group, tq, hd = q_ref.shape
    n_kt, _, tk = vt_ref.shape
    nq = group * tq
    s_scr, p_scr = slots[:ATTN_SLOTS], slots[ATTN_SLOTS:]
    st_idx = jnp.minimum(pl.program_id(2), 0)
    ld_idx = jnp.minimum(pl.program_id(1), 0)

    def scores(kt, s_scr):
        k_tile = k_ref[pl.ds(pl.multiple_of(kt * tk, tk), tk), :]
        s = lax.dot_general(k_tile, q_ref[...].reshape(nq, hd), (((1,), (1,)), ((), ())),
                            preferred_element_type=F32)
        s_scr[st_idx] = s
        return jnp.max(s, axis=0, keepdims=True)

    def softmax(s_scr, p_scr, m, tile_max):
        m_new = jnp.maximum(m, tile_max)
        p_scr[st_idx] = jnp.exp2(s_scr[ld_idx] - m_new).astype(BF16)
        return m_new, jnp.exp2(m - m_new)

    def values(kt, p_scr, alpha):
        pv = jnp.dot(vt_ref[kt], p_scr[ld_idx], preferred_element_type=F32)
        acc_scr[...] = alpha * acc_scr[...] + pv

    def tile(t, slot, carry, prefetch):
        m, max_t, max_t1, alpha_t2, alpha_t1 = carry
        max_t2 = scores(t + 2, s_scr[(slot + 2) % ATTN_SLOTS]) if prefetch else None
        if alpha_t2 is not None:
            values(t - 2, p_scr[(slot - 2) % ATTN_SLOTS], alpha_t2)
        m, alpha = softmax(s_scr[slot], p_scr[slot], m, max_t)
        return m, max_t1, max_t2, alpha_t1, alpha

    def group_of_tiles(t0, carry, n_prefetch=ATTN_SLOTS):
        for slot in range(ATTN_SLOTS):
            carry = tile(t0 + slot, slot, carry, prefetch=slot < n_prefetch)
        return carry

    n_groups = n_kt // ATTN_SLOTS
    acc_scr[...] = jnp.zeros_like(acc_scr)
    m0 = jnp.full((1, nq), -jnp.inf, F32)
    carry = (m0, scores(0, s_scr[0]), scores(1, s_scr[1]), None, None)
    carry = group_of_tiles(0, carry)
    carry = lax.fori_loop(1, n_groups - 1, lambda g, c: group_of_tiles(g * ATTN_SLOTS, c), carry,
                          unroll=True)
    _, _, _, alpha_t2, alpha_t1 = group_of_tiles(n_kt - ATTN_SLOTS, carry,
                                                 n_prefetch=ATTN_SLOTS - 2)
    values(n_kt - 2, p_scr[ATTN_SLOTS - 2], alpha_t2)
    values(n_kt - 1, p_scr[ATTN_SLOTS - 1], alpha_t1)
    out_t = acc_scr[:hd, :] / acc_scr[hd:hd + 1, :]
    for g in range(group):
        o_ref[:, g * hd:(g + 1) * hd] = out_t[:, g * tq:(g + 1) * tq].T.astype(BF16)


def _attn_bounded_kernel(q_ref, k_ref, vt_ref, o_ref, acc_scr, *p_scr):
    group, tq, hd = q_ref.shape
    n_kt, _, tk = vt_ref.shape
    nq = group * tq
    st_idx = jnp.minimum(pl.program_id(2), 0)
    ld_idx = jnp.minimum(pl.program_id(1), 0)

    def probabilities(kt):
        k_tile = k_ref[kt * tk:(kt + 1) * tk, :]
        s = lax.dot_general(k_tile, q_ref[...].reshape(nq, hd), (((1,), (1,)), ((), ())),
                            preferred_element_type=F32)
        p_scr[kt % ATTN_SLOTS][st_idx] = jnp.exp2(s).astype(BF16)

    def values(kt):
        acc_scr[...] += jnp.dot(vt_ref[kt], p_scr[kt % ATTN_SLOTS][ld_idx],
                                preferred_element_type=F32)

    acc_scr[...] = jnp.zeros_like(acc_scr)
    probabilities(0)
    probabilities(1)
    for kt in range(n_kt):
        if kt + 2 < n_kt:
            probabilities(kt + 2)
        values(kt)
    out_t = acc_scr[:hd, :] / acc_scr[hd:hd + 1, :]
    for g in range(group):
        o_ref[:, g * hd:(g + 1) * hd] = out_t[:, g * tq:(g + 1) * tq].T.astype(BF16)


MAX_UNSTABILISED_SCORE = 60.0


def _attention(q, k, vt, score_bound, tq=128):
    bsz, n_kv, group, s, hd = q.shape
    n_kt, tk = vt.shape[2], vt.shape[4]
    assert n_kt % ATTN_SLOTS == 0 and n_kt >= 2 * ATTN_SLOTS
    nq = group * tq

    def call(body, scratch, name):
        return pl.pallas_call(
            body,
            grid=(bsz, n_kv, s // tq),
            in_specs=[
                pl.BlockSpec((None, None, group, tq, hd), lambda b, h, i: (b, h, 0, i, 0)),
                pl.BlockSpec((None, None, s, hd), lambda b, h, i: (b, h, 0, 0)),
                pl.BlockSpec((None, None, n_kt, vt.shape[3], tk), lambda b, h, i: (b, h, 0, 0, 0)),
            ],
            out_specs=pl.BlockSpec((None, tq, group * hd), lambda b, h, i: (b, i, h)),
            out_shape=jax.ShapeDtypeStruct((bsz, s, n_kv * group * hd), BF16),
            scratch_shapes=[pltpu.VMEM((vt.shape[3], nq), F32)] + scratch,
            compiler_params=_compiler_params(3),
            name=name,
        )(q, k, vt)

    p_slots = [pltpu.VMEM((1, tk, nq), BF16)] * ATTN_SLOTS
    s_slots = [pltpu.VMEM((1, tk, nq), F32)] * ATTN_SLOTS
    return lax.cond(
        score_bound <= MAX_UNSTABILISED_SCORE,
        lambda: call(_attn_bounded_kernel, p_slots, "attention_bounded"),
        lambda: call(_attn_kernel, s_slots + p_slots, "attention"),
    )


def _oproj_kernel(a_ref, w_ref, x_ref, mod_ref, o_ref):
    y = jnp.dot(a_ref[...], w_ref[...], preferred_element_type=F32)
    o_ref[...] = x_ref[...] + mod_ref[2:3, :] * y


def _oproj(a, w_o, x, mod, tm=512):
    bsz, s, d = x.shape
    return pl.pallas_call(
        _oproj_kernel,
        grid=(bsz, s // tm),
        in_specs=[
            pl.BlockSpec((None, tm, a.shape[-1]), lambda b, i: (b, i, 0)),
            pl.BlockSpec(w_o.shape, lambda b, i: (0, 0)),
            pl.BlockSpec((None, tm, d), lambda b, i: (b, i, 0)),
            pl.BlockSpec((None, 6, d), lambda b, i: (b, 0, 0)),
        ],
        out_specs=pl.BlockSpec((None, tm, d), lambda b, i: (b, i, 0)),
        out_shape=jax.ShapeDtypeStruct((bsz, s, d), F32),
        compiler_params=_compiler_params(2),
        name="oproj",
    )(a, w_o, x, mod)


def _ffn_kernel(x_ref, xp_ref, xn_ref, mod_ref, g_ref, wg_ref, wv_ref, cwg_ref, cwv_ref,
                cbg_ref, cbv_ref, wd_ref, gf_ref, o_ref, h_scr, ag_scr, av_scr, *, final_norm):
    tm = x_ref.shape[0]
    halo = xp_ref.shape[0]
    i = pl.program_id(1)
    j = pl.program_id(2)

    @pl.when(j == 0)
    def _():
        gain, scale, shift = g_ref[...], mod_ref[4:5, :], mod_ref[3:4, :]
        h_scr[halo:halo + tm, :] = _modnorm(x_ref[...], gain, scale, shift).astype(BF16)
        hp = _modnorm(xp_ref[...], gain, scale, shift)
        hn = _modnorm(xn_ref[...], gain, scale, shift)
        h_scr[0:halo, :] = jnp.where(i == 0, 0.0, hp).astype(BF16)
        h_scr[halo + tm:, :] = jnp.where(i == pl.num_programs(1) - 1, 0.0, hn).astype(BF16)
        o_ref[...] = jnp.zeros_like(o_ref)

    def conv(a_scr, cw_ref, cb_ref):
        return (cw_ref[0:1, :] * a_scr[halo - 1:halo - 1 + tm, :]
                + cw_ref[1:2, :] * a_scr[halo:halo + tm, :]
                + cw_ref[2:3, :] * a_scr[halo + 1:halo + 1 + tm, :]
                + cb_ref[...])

    ag_scr[...] = jnp.dot(h_scr[...], wg_ref[...], preferred_element_type=F32)
    av_scr[...] = jnp.dot(h_scr[...], wv_ref[...], preferred_element_type=F32)
    act = (_gelu(conv(ag_scr, cwg_ref, cbg_ref)) * conv(av_scr, cwv_ref, cbv_ref)).astype(BF16)
    o_ref[...] += jnp.dot(act, wd_ref[...], preferred_element_type=F32)

    @pl.when(j == pl.num_programs(2) - 1)
    def _():
        y = x_ref[...] + mod_ref[5:6, :] * o_ref[...]
        if final_norm:
            y = (y * _rms_scale(y)) * gf_ref[...]
        o_ref[...] = y


def _ffn(x, mod, g, w_up, conv_w, conv_b, w_down, g_final, final_norm, tm=512, tf=512):
    bsz, s, d = x.shape
    dff = w_down.shape[0]
    nj = dff // tf
    halo = V7X_SUBLANES
    nhb = tm // halo
    last_hb = s // halo - 1
    kernel = functools.partial(_ffn_kernel, final_norm=final_norm)
    return pl.pallas_call(
        kernel,
        grid=(bsz, s // tm, nj),
        in_specs=[
            pl.BlockSpec((None, tm, d), lambda b, i, j: (b, i, 0)),
            pl.BlockSpec((None, halo, d), lambda b, i, j: (b, jnp.maximum(i * nhb - 1, 0), 0)),
            pl.BlockSpec((None, halo, d), lambda b, i, j: (b, jnp.minimum((i + 1) * nhb, last_hb), 0)),
            pl.BlockSpec((None, 6, d), lambda b, i, j: (b, 0, 0)),
            pl.BlockSpec((1, d), lambda b, i, j: (0, 0)),
            pl.BlockSpec((d, tf), lambda b, i, j: (0, j)),
            pl.BlockSpec((d, tf), lambda b, i, j: (0, nj + j)),
            pl.BlockSpec((3, tf), lambda b, i, j: (0, j)),
            pl.BlockSpec((3, tf), lambda b, i, j: (0, nj + j)),
            pl.BlockSpec((1, tf), lambda b, i, j: (0, j)),
            pl.BlockSpec((1, tf), lambda b, i, j: (0, nj + j)),
            pl.BlockSpec((tf, d), lambda b, i, j: (j, 0)),
            pl.BlockSpec((1, d), lambda b, i, j: (0, 0)),
        ],
        out_specs=pl.BlockSpec((None, tm, d), lambda b, i, j: (b, i, 0)),
        out_shape=jax.ShapeDtypeStruct((bsz, s, d), F32),
        scratch_shapes=[
            pltpu.VMEM((tm + 2 * halo, d), BF16),
            pltpu.VMEM((tm + 2 * halo, tf), F32),
            pltpu.VMEM((tm + 2 * halo, tf), F32),
        ],
        compiler_params=_compiler_params(3),
        name="ffn",
    )(x, x, x, mod, g.reshape(1, d), w_up, w_up, conv_w, conv_w,
      conv_b.reshape(1, -1), conv_b.reshape(1, -1), w_down, g_final.reshape(1, d))


def _rope_tables(s, hd):
    axis_dim = hd // 2
    pos = jnp.arange(s)
    inv_freq = ROPE_THETA ** (-jnp.arange(0, axis_dim, 2, dtype=F32) / axis_dim)
    ang_r = (pos // GRID_W).astype(F32)[:, None] * inv_freq[None, :]
    ang_c = (pos % GRID_W).astype(F32)[:, None] * inv_freq[None, :]
    cos = jnp.concatenate([jnp.cos(ang_r)] * 2 + [jnp.cos(ang_c)] * 2, axis=-1)
    sin = jnp.concatenate([-jnp.sin(ang_r), jnp.sin(ang_r), -jnp.sin(ang_c), jnp.sin(ang_c)], axis=-1)
    return cos, sin


def kernel(x, c, w_ada, b_ada, g_norm, g_final, a_w_in, a_g_v, a_w_s, a_b_s, a_w_out,
           b_w_qkv, b_g_q, b_g_k, b_w_o, f_w_up, f_conv_w, f_conv_b, f_w_down):
    depth = w_ada.shape[0]
    bsz, s, d = x.shape
    hd = b_g_q.shape[-1]
    n_heads = d // hd
    n_kv = (b_w_qkv.shape[-1] // hd - n_heads) // 2
    n_mixers = 2

    mod = _adaln(c, w_ada, b_ada)
    cos, sin = _rope_tables(s, hd)

    for i in range(depth):
        j = i // n_mixers
        if i % n_mixers == 0:
            n_groups, chunk, _ = a_w_s.shape[1:]
            width = a_w_out.shape[1]
            bs_full = jnp.repeat(jnp.transpose(a_b_s[j]), width // n_groups, axis=1)
            u, vn = _gmlp_in(x, mod[i], g_norm[i, 0], a_w_in[j].astype(BF16), a_g_v[j])
            x = _gmlp_out(u, vn, a_w_s[j].astype(BF16), bs_full, x, mod[i], a_w_out[j].astype(BF16))
        else:
            w = b_w_qkv[j]
            qk_cols = (n_heads + n_kv) * hd
            q, k, vt = _qkv(x, mod[i], g_norm[i, 0], w[:, :qk_cols].astype(BF16),
                            jnp.transpose(w[:, qk_cols:]).astype(BF16), b_g_q[j], b_g_k[j],
                            cos, sin, n_kv)
            score_bound = (1.01 * math.log2(math.e) * math.sqrt(hd)
                           * jnp.max(jnp.abs(b_g_q[j])) * jnp.max(jnp.abs(b_g_k[j])))
            a = _attention(q, k, vt, score_bound)
            x = _oproj(a, b_w_o[j].astype(BF16), x, mod[i])
        x = _ffn(x, mod[i], g_norm[i, 1], f_w_up[i].astype(BF16), f_conv_w[i], f_conv_b[i],
                 f_w_down[i].astype(BF16), g_final, final_norm=(i == depth - 1))
    return x
```

```python
import functools
import math

import jax
import jax.numpy as jnp
from jax import lax
from jax.experimental import pallas as pl
from jax.experimental.pallas import tpu as pltpu

EPS = 1e-6
GRID_W = 64
ROPE_THETA = 10000.0
V7X_VMEM_BYTES = 64 * 1024 * 1024
V7X_SUBLANES = 8
V7X_BF16_SUBLANES = 16
VMEM_LIMIT_BYTES = V7X_VMEM_BYTES - 8 * 1024 * 1024

BF16 = jnp.bfloat16
F32 = jnp.float32


def _compiler_params(n_grid_axes):
    return pltpu.CompilerParams(
        dimension_semantics=("arbitrary",) * n_grid_axes,
        vmem_limit_bytes=VMEM_LIMIT_BYTES,
    )


def _rms_scale(x):
    return lax.rsqrt(jnp.mean(x * x, axis=-1, keepdims=True) + EPS)


def _modnorm(x, gain, scale, shift):
    return (x * _rms_scale(x)) * (gain * (1.0 + scale)) + shift


def _gelu(x):
    return 0.5 * x * (1.0 + lax.erf(x * (1.0 / math.sqrt(2.0))))


def _adaln_kernel(c_ref, w_ref, b_ref, o_ref):
    c = c_ref[...]
    cond = c * (1.0 / (1.0 + jnp.exp(-c)))
    o_ref[...] = jnp.dot(cond, w_ref[...], precision=lax.Precision.HIGHEST,
                         preferred_element_type=F32) + b_ref[...]


def _adaln(c, w_ada, b_ada, tn=1024):
    depth, d, n = w_ada.shape
    bsz = c.shape[0]
    rows = -(-bsz // V7X_SUBLANES) * V7X_SUBLANES
    c_pad = jnp.zeros((rows, d), F32).at[:bsz].set(c)
    out = pl.pallas_call(
        _adaln_kernel,
        grid=(depth, n // tn),
        in_specs=[
            pl.BlockSpec((rows, d), lambda l, j: (0, 0)),
            pl.BlockSpec((None, d, tn), lambda l, j: (l, 0, j)),
            pl.BlockSpec((None, 1, tn), lambda l, j: (l, 0, j)),
        ],
        out_specs=pl.BlockSpec((None, rows, tn), lambda l, j: (l, 0, j)),
        out_shape=jax.ShapeDtypeStruct((depth, rows, n), F32),
        compiler_params=_compiler_params(2),
        name="adaln",
    )(c_pad, w_ada, b_ada.reshape(depth, 1, n))
    return out[:, :bsz].reshape(depth, bsz, 6, d)


def _gmlp_in_kernel(x_ref, mod_ref, g_ref, w_ref, gv_ref, u_ref, vn_ref, h_scr, v_scr, *, nc):
    width = u_ref.shape[-1]
    h_scr[...] = _modnorm(x_ref[...], g_ref[...], mod_ref[1:2, :], mod_ref[0:1, :]).astype(BF16)
    ssq = jnp.zeros((x_ref.shape[0], 1), F32)
    for c in range(width // nc):
        cols = slice(c * nc, (c + 1) * nc)
        zu = _gelu(jnp.dot(h_scr[...], w_ref[:, cols], preferred_element_type=F32))
        u_ref[:, cols] = zu.astype(BF16)
        zv = _gelu(jnp.dot(h_scr[...], w_ref[:, width + c * nc: width + (c + 1) * nc],
                           preferred_element_type=F32))
        v_scr[:, cols] = zv
        ssq = ssq + jnp.sum(zv * zv, axis=-1, keepdims=True)
    r = lax.rsqrt(ssq * (1.0 / width) + EPS)
    vn_ref[...] = ((v_scr[...] * r) * gv_ref[...]).astype(BF16)


def _gmlp_in(x, mod, g, w_in, g_v, tm=512, nc=512):
    bsz, s, d = x.shape
    width = w_in.shape[1] // 2
    kernel = functools.partial(_gmlp_in_kernel, nc=nc)
    return pl.pallas_call(
        kernel,
        grid=(bsz, s // tm),
        in_specs=[
            pl.BlockSpec((None, tm, d), lambda b, i: (b, i, 0)),
            pl.BlockSpec((None, 6, d), lambda b, i: (b, 0, 0)),
            pl.BlockSpec((1, d), lambda b, i: (0, 0)),
            pl.BlockSpec((d, 2 * width), lambda b, i: (0, 0)),
            pl.BlockSpec((1, width), lambda b, i: (0, 0)),
        ],
        out_specs=[
            pl.BlockSpec((None, tm, width), lambda b, i: (b, i, 0)),
            pl.BlockSpec((None, tm, width), lambda b, i: (b, i, 0)),
        ],
        out_shape=[jax.ShapeDtypeStruct((bsz, s, width), BF16)] * 2,
        scratch_shapes=[pltpu.VMEM((tm, d), BF16), pltpu.VMEM((tm, width), F32)],
        compiler_params=_compiler_params(2),
        name="gmlp_in",
    )(x, mod, g.reshape(1, d), w_in, g_v.reshape(1, width))


def _gmlp_out_kernel(u_ref, vn_ref, ws_ref, bs_ref, x_ref, mod_ref, wo_ref, o_ref, y_scr):
    n_groups, chunk, _ = ws_ref.shape
    gd = u_ref.shape[-1] // n_groups
    for n in range(u_ref.shape[0] // chunk):
        rows = slice(n * chunk, (n + 1) * chunk)
        for g in range(n_groups):
            cols = slice(g * gd, (g + 1) * gd)
            sv = jnp.dot(ws_ref[g], vn_ref[rows, cols], preferred_element_type=F32) + bs_ref[:, cols]
            y_scr[rows, cols] = (u_ref[rows, cols].astype(F32) * sv).astype(BF16)
    y = jnp.dot(y_scr[...], wo_ref[...], preferred_element_type=F32)
    o_ref[...] = x_ref[...] + mod_ref[2:3, :] * y


def _gmlp_out(u, vn, w_s, bs_full, x, mod, w_out, tm=512):
    bsz, s, d = x.shape
    width = u.shape[-1]
    n_groups, chunk, _ = w_s.shape
    return pl.pallas_call(
        _gmlp_out_kernel,
        grid=(bsz, s // tm),
        in_specs=[
            pl.BlockSpec((None, tm, width), lambda b, i: (b, i, 0)),
            pl.BlockSpec((None, tm, width), lambda b, i: (b, i, 0)),
            pl.BlockSpec((n_groups, chunk, chunk), lambda b, i: (0, 0, 0)),
            pl.BlockSpec((chunk, width), lambda b, i: (0, 0)),
            pl.BlockSpec((None, tm, d), lambda b, i: (b, i, 0)),
            pl.BlockSpec((None, 6, d), lambda b, i: (b, 0, 0)),
            pl.BlockSpec((width, d), lambda b, i: (0, 0)),
        ],
        out_specs=pl.BlockSpec((None, tm, d), lambda b, i: (b, i, 0)),
        out_shape=jax.ShapeDtypeStruct((bsz, s, d), F32),
        scratch_shapes=[pltpu.VMEM((tm, width), BF16)],
        compiler_params=_compiler_params(2),
        name="gmlp_out",
    )(u, vn, w_s, bs_full, x, mod, w_out)


def _rope(t, cos, sin_signed, half_idx):
    quarter = t.shape[-1] // 4
    partner = jnp.where(half_idx, pltpu.roll(t, t.shape[-1] - quarter, 1), pltpu.roll(t, quarter, 1))
    return t * cos + partner * sin_signed


def _qkv_kernel(x_ref, mod_ref, g_ref, wqk_ref, wvt_ref, gq_ref, gk_ref, cos_ref, sin_ref,
                q_ref, k_ref, vt_ref, h_scr, *, q_scale):
    n_kv, group, tm, hd = q_ref.shape
    h_scr[...] = _modnorm(x_ref[...], g_ref[...], mod_ref[1:2, :], mod_ref[0:1, :]).astype(BF16)
    cos = cos_ref[...]
    sin = sin_ref[...]
    lane = lax.broadcasted_iota(jnp.int32, (tm, hd), 1)
    half_idx = (lane % (hd // 2)) < (hd // 4)

    def head(t, gain, post):
        t = t * lax.rsqrt(jnp.mean(t * t, axis=-1, keepdims=True) + EPS) * gain
        return (_rope(t, cos, sin, half_idx) * post).astype(BF16)

    for kv in range(n_kv):
        cols = slice(kv * group * hd, (kv + 1) * group * hd)
        tq = jnp.dot(h_scr[...], wqk_ref[:, cols], preferred_element_type=F32)
        for g in range(group):
            q_ref[kv, g] = head(tq[:, g * hd:(g + 1) * hd], gq_ref[...], q_scale)
    k_off = n_kv * group * hd
    tk = jnp.dot(h_scr[...], wqk_ref[:, k_off:k_off + n_kv * hd], preferred_element_type=F32)
    for kv in range(n_kv):
        k_ref[kv] = head(tk[:, kv * hd:(kv + 1) * hd], gk_ref[...], 1.0)
    vt = lax.dot_general(wvt_ref[...], h_scr[...], (((1,), (1,)), ((), ())),
                         preferred_element_type=F32)
    pad_rows = vt_ref.shape[2] - hd
    ones_row = lax.broadcasted_iota(jnp.int32, (pad_rows, tm), 0) == 0
    for kv in range(n_kv):
        vt_ref[kv, 0, :hd, :] = vt[kv * hd:(kv + 1) * hd, :].astype(BF16)
        vt_ref[kv, 0, hd:, :] = ones_row.astype(BF16)


def _qkv(x, mod, g, w_qk, w_vt, g_q, g_k, cos, sin, n_kv, tm=512):
    bsz, s, d = x.shape
    hd = g_q.shape[-1]
    n_heads = d // hd
    group = n_heads // n_kv
    kernel = functools.partial(_qkv_kernel, q_scale=math.log2(math.e) / math.sqrt(hd))
    return pl.pallas_call(
        kernel,
        grid=(bsz, s // tm),
        in_specs=[
            pl.BlockSpec((None, tm, d), lambda b, i: (b, i, 0)),
            pl.BlockSpec((None, 6, d), lambda b, i: (b, 0, 0)),
            pl.BlockSpec((1, d), lambda b, i: (0, 0)),
            pl.BlockSpec(w_qk.shape, lambda b, i: (0, 0)),
            pl.BlockSpec(w_vt.shape, lambda b, i: (0, 0)),
            pl.BlockSpec((1, hd), lambda b, i: (0, 0)),
            pl.BlockSpec((1, hd), lambda b, i: (0, 0)),
            pl.BlockSpec((tm, hd), lambda b, i: (i, 0)),
            pl.BlockSpec((tm, hd), lambda b, i: (i, 0)),
        ],
        out_specs=[
            pl.BlockSpec((None, n_kv, group, tm, hd), lambda b, i: (b, 0, 0, i, 0)),
            pl.BlockSpec((None, n_kv, tm, hd), lambda b, i: (b, 0, i, 0)),
            pl.BlockSpec((None, n_kv, 1, hd + V7X_BF16_SUBLANES, tm), lambda b, i: (b, 0, i, 0, 0)),
        ],
        out_shape=[
            jax.ShapeDtypeStruct((bsz, n_kv, group, s, hd), BF16),
            jax.ShapeDtypeStruct((bsz, n_kv, s, hd), BF16),
            jax.ShapeDtypeStruct((bsz, n_kv, s // tm, hd + V7X_BF16_SUBLANES, tm), BF16),
        ],
        scratch_shapes=[pltpu.VMEM((tm, d), BF16)],
        compiler_params=_compiler_params(2),
        name="qkv",
    )(x, mod, g.reshape(1, d), w_qk, w_vt, g_q.reshape(1, hd), g_k.reshape(1, hd), cos, sin)


ATTN_SLOTS = 4


def _attn_kernel(q_ref, k_ref, vt_ref, o_ref, acc_scr, *slots):
    group, tq, hd = q_ref.shape
    n_kt, _, tk = vt_ref.shape
    nq = group * tq
    s_scr, p_scr = slots[:ATTN_SLOTS], slots[ATTN_SLOTS:]
    st_idx = jnp.minimum(pl.program_id(2), 0)
    ld_idx = jnp.minimum(pl.program_id(1), 0)

    def scores(kt, s_scr):
        k_tile = k_ref[pl.ds(pl.multiple_of(kt * tk, tk), tk), :]
        s = lax.dot_general(k_tile, q_ref[...].reshape(nq, hd), (((1,), (1,)), ((), ())),
                            preferred_element_type=F32)
        s_scr[st_idx] = s
        return jnp.max(s, axis=0, keepdims=True)

    def softmax(s_scr, p_scr, m, tile_max):
        m_new = jnp.maximum(m, tile_max)
        p_scr[st_idx] = jnp.exp2(s_scr[ld_idx] - m_new).astype(BF16)
        return m_new, jnp.exp2(m - m_new)

    def values(kt, p_scr, alpha):
        pv = jnp.dot(vt_ref[kt], p_scr[ld_idx], preferred_element_type=F32)
        acc_scr[...] = alpha * acc_scr[...] + pv

    def tile(t, slot, carry, prefetch):
        m, max_t, max_t1, alpha_t2, alpha_t1 = carry
        max_t2 = scores(t + 2, s_scr[(slot + 2) % ATTN_SLOTS]) if prefetch else None
        if alpha_t2 is not None:
            values(t - 2, p_scr[(slot - 2) % ATTN_SLOTS], alpha_t2)
        m, alpha = softmax(s_scr[slot], p_scr[slot], m, max_t)
        return m, max_t1, max_t2, alpha_t1, alpha

    def group_of_tiles(t0, carry, n_prefetch=ATTN_SLOTS):
        for slot in range(ATTN_SLOTS):
            carry = tile(t0 + slot, slot, carry, prefetch=slot < n_prefetch)
        return carry

    n_groups = n_kt // ATTN_SLOTS
    acc_scr[...] = jnp.zeros_like(acc_scr)
    m0 = jnp.full((1, nq), -jnp.inf, F32)
    carry = (m0, scores(0, s_scr[0]), scores(1, s_scr[1]), None, None)
    carry = group_of_tiles(0, carry)
    carry = lax.fori_loop(1, n_groups - 1, lambda g, c: group_of_tiles(g * ATTN_SLOTS, c), carry,
                          unroll=True)
    _, _, _, alpha_t2, alpha_t1 = group_of_tiles(n_kt - ATTN_SLOTS, carry,
                                                 n_prefetch=ATTN_SLOTS - 2)
    values(n_kt - 2, p_scr[ATTN_SLOTS - 2], alpha_t2)
    values(n_kt - 1, p_scr[ATTN_SLOTS - 1], alpha_t1)
    out_t = acc_scr[:hd, :] / acc_scr[hd:hd + 1, :]
    for g in range(group):
        o_ref[:, g * hd:(g + 1) * hd] = out_t[:, g * tq:(g + 1) * tq].T.astype(BF16)


def _attn_bounded_kernel(q_ref, k_ref, vt_ref, o_ref, acc_scr, *p_scr):
    group, tq, hd = q_ref.shape
    n_kt, _, tk = vt_ref.shape
    nq = group * tq
    st_idx = jnp.minimum(pl.program_id(2), 0)
    ld_idx = jnp.minimum(pl.program_id(1), 0)

    def probabilities(kt):
        k_tile = k_ref[kt * tk:(kt + 1) * tk, :]
        s = lax.dot_general(k_tile, q_ref[...].reshape(nq, hd), (((1,), (1,)), ((), ())),
                            preferred_element_type=F32)
        p = jnp.exp2(s)
        p_scr[kt % ATTN_SLOTS][st_idx] = p.astype(BF16)
        return jnp.sum(p, axis=0, keepdims=True)

    def values(kt):
        acc_scr[:hd, :] += jnp.dot(vt_ref[kt, :hd, :], p_scr[kt % ATTN_SLOTS][ld_idx],
                                   preferred_element_type=F32)

    acc_scr[...] = jnp.zeros_like(acc_scr)
    l = probabilities(0) + probabilities(1)
    for kt in range(n_kt):
        if kt + 2 < n_kt:
            l = l + probabilities(kt + 2)
        values(kt)
    out_t = acc_scr[:hd, :] / l
    for g in range(group):
        o_ref[:, g * hd:(g + 1) * hd] = out_t[:, g * tq:(g + 1) * tq].T.astype(BF16)


MAX_UNSTABILISED_SCORE = 60.0


def _attention(q, k, vt, score_bound, tq=128):
    bsz, n_kv, group, s, hd = q.shape
    n_kt, tk = vt.shape[2], vt.shape[4]
    assert n_kt % ATTN_SLOTS == 0 and n_kt >= 2 * ATTN_SLOTS
    nq = group * tq

    def call(body, scratch, name):
        return pl.pallas_call(
            body,
            grid=(bsz, n_kv, s // tq),
            in_specs=[
                pl.BlockSpec((None, None, group, tq, hd), lambda b, h, i: (b, h, 0, i, 0)),
                pl.BlockSpec((None, None, s, hd), lambda b, h, i: (b, h, 0, 0)),
                pl.BlockSpec((None, None, n_kt, vt.shape[3], tk), lambda b, h, i: (b, h, 0, 0, 0)),
            ],
            out_specs=pl.BlockSpec((None, tq, group * hd), lambda b, h, i: (b, i, h)),
            out_shape=jax.ShapeDtypeStruct((bsz, s, n_kv * group * hd), BF16),
            scratch_shapes=[pltpu.VMEM((vt.shape[3], nq), F32)] + scratch,
            compiler_params=_compiler_params(3),
            name=name,
        )(q, k, vt)

    p_slots = [pltpu.VMEM((1, tk, nq), BF16)] * ATTN_SLOTS
    s_slots = [pltpu.VMEM((1, tk, nq), F32)] * ATTN_SLOTS
    return lax.cond(
        score_bound <= MAX_UNSTABILISED_SCORE,
        lambda: call(_attn_bounded_kernel, p_slots, "attention_bounded"),
        lambda: call(_attn_kernel, s_slots + p_slots, "attention"),
    )


def _oproj_kernel(a_ref, w_ref, x_ref, mod_ref, o_ref):
    y = jnp.dot(a_ref[...], w_ref[...], preferred_element_type=F32)
    o_ref[...] = x_ref[...] + mod_ref[2:3, :] * y


def _oproj(a, w_o, x, mod, tm=512):
    bsz, s, d = x.shape
    return pl.pallas_call(
        _oproj_kernel,
        grid=(bsz, s // tm),
        in_specs=[
            pl.BlockSpec((None, tm, a.shape[-1]), lambda b, i: (b, i, 0)),
            pl.BlockSpec(w_o.shape, lambda b, i: (0, 0)),
            pl.BlockSpec((None, tm, d), lambda b, i: (b, i, 0)),
            pl.BlockSpec((None, 6, d), lambda b, i: (b, 0, 0)),
        ],
        out_specs=pl.BlockSpec((None, tm, d), lambda b, i: (b, i, 0)),
        out_shape=jax.ShapeDtypeStruct((bsz, s, d), F32),
        compiler_params=_compiler_params(2),
        name="oproj",
    )(a, w_o, x, mod)


def _ffn_kernel(x_ref, xp_ref, xn_ref, mod_ref, g_ref, wg_ref, wv_ref, cwg_ref, cwv_ref,
                cbg_ref, cbv_ref, wd_ref, gf_ref, o_ref, h_scr, ag_scr, av_scr, *, final_norm, n_sub):
    tm = x_ref.shape[0]
    halo = xp_ref.shape[0]
    i = pl.program_id(1)
    j = pl.program_id(2)

    @pl.when(j == 0)
    def _():
        gain, scale, shift = g_ref[...], mod_ref[4:5, :], mod_ref[3:4, :]
        h_scr[halo:halo + tm, :] = _modnorm(x_ref[...], gain, scale, shift).astype(BF16)
        hp = _modnorm(xp_ref[...], gain, scale, shift)
        hn = _modnorm(xn_ref[...], gain, scale, shift)
        h_scr[0:halo, :] = jnp.where(i == 0, 0.0, hp).astype(BF16)
        h_scr[halo + tm:, :] = jnp.where(i == pl.num_programs(1) - 1, 0.0, hn).astype(BF16)
        o_ref[...] = jnp.zeros_like(o_ref)

    def conv(a_scr, cw_ref, cb_ref, cols):
        return (cw_ref[0:1, cols] * a_scr[halo - 1:halo - 1 + tm, cols]
                + cw_ref[1:2, cols] * a_scr[halo:halo + tm, cols]
                + cw_ref[2:3, cols] * a_scr[halo + 1:halo + 1 + tm, cols]
                + cb_ref[:, cols])

    tf = wg_ref.shape[1]
    chunks = [slice(c * (tf // n_sub), (c + 1) * (tf // n_sub)) for c in range(n_sub)]
    for cols in chunks:
        ag_scr[:, cols] = jnp.dot(h_scr[...], wg_ref[:, cols], preferred_element_type=F32)
        av_scr[:, cols] = jnp.dot(h_scr[...], wv_ref[:, cols], preferred_element_type=F32)
    for cols in chunks:
        act = (_gelu(conv(ag_scr, cwg_ref, cbg_ref, cols))
               * conv(av_scr, cwv_ref, cbv_ref, cols)).astype(BF16)
        o_ref[...] += jnp.dot(act, wd_ref[cols, :], preferred_element_type=F32)

    @pl.when(j == pl.num_programs(2) - 1)
    def _():
        y = x_ref[...] + mod_ref[5:6, :] * o_ref[...]
        if final_norm:
            y = (y * _rms_scale(y)) * gf_ref[...]
        o_ref[...] = y


def _ffn(x, mod, g, layer, w_up, conv_w, conv_b, w_down, g_final, final_norm,
         tm=512, tf=512, n_sub=1):
    bsz, s, d = x.shape
    dff = w_down.shape[1]
    nj = dff // tf
    halo = V7X_SUBLANES
    nhb = tm // halo
    last_hb = s // halo - 1
    kernel = functools.partial(_ffn_kernel, final_norm=final_norm, n_sub=n_sub)
    return pl.pallas_call(
        kernel,
        grid=(bsz, s // tm, nj),
        in_specs=[
            pl.BlockSpec((None, tm, d), lambda b, i, j: (b, i, 0)),
            pl.BlockSpec((None, halo, d), lambda b, i, j: (b, jnp.maximum(i * nhb - 1, 0), 0)),
            pl.BlockSpec((None, halo, d), lambda b, i, j: (b, jnp.minimum((i + 1) * nhb, last_hb), 0)),
            pl.BlockSpec((None, 6, d), lambda b, i, j: (b, 0, 0)),
            pl.BlockSpec((1, d), lambda b, i, j: (0, 0)),
            pl.BlockSpec((None, d, tf), lambda b, i, j: (layer, 0, j)),
            pl.BlockSpec((None, d, tf), lambda b, i, j: (layer, 0, nj + j)),
            pl.BlockSpec((None, 3, tf), lambda b, i, j: (layer, 0, j)),
            pl.BlockSpec((None, 3, tf), lambda b, i, j: (layer, 0, nj + j)),
            pl.BlockSpec((None, 1, tf), lambda b, i, j: (layer, 0, j)),
            pl.BlockSpec((None, 1, tf), lambda b, i, j: (layer, 0, nj + j)),
            pl.BlockSpec((None, tf, d), lambda b, i, j: (layer, j, 0)),
            pl.BlockSpec((1, d), lambda b, i, j: (0, 0)),
        ],
        out_specs=pl.BlockSpec((None, tm, d), lambda b, i, j: (b, i, 0)),
        out_shape=jax.ShapeDtypeStruct((bsz, s, d), F32),
        scratch_shapes=[
            pltpu.VMEM((tm + 2 * halo, d), BF16),
            pltpu.VMEM((tm + 2 * halo, tf), F32),
            pltpu.VMEM((tm + 2 * halo, tf), F32),
        ],
        compiler_params=_compiler_params(3),
        name="ffn",
    )(x, x, x, mod, g.reshape(1, d), w_up, w_up, conv_w, conv_w,
      conv_b[:, None, :], conv_b[:, None, :], w_down, g_final.reshape(1, d))


def _rope_tables(s, hd):
    axis_dim = hd // 2
    n_rows = s // GRID_W
    inv_freq = ROPE_THETA ** (-jnp.arange(0, axis_dim, 2, dtype=F32) / axis_dim)
    ang_r = jnp.arange(n_rows).astype(F32)[:, None] * inv_freq[None, :]
    ang_c = jnp.arange(GRID_W).astype(F32)[:, None] * inv_freq[None, :]
    by_row = lambda t: jnp.repeat(t, GRID_W, axis=0)
    by_col = lambda t: jnp.tile(t, (n_rows, 1))
    cos_r, sin_r = by_row(jnp.cos(ang_r)), by_row(jnp.sin(ang_r))
    cos_c, sin_c = by_col(jnp.cos(ang_c)), by_col(jnp.sin(ang_c))
    cos = jnp.concatenate([cos_r, cos_r, cos_c, cos_c], axis=-1)
    sin = jnp.concatenate([-sin_r, sin_r, -sin_c, sin_c], axis=-1)
    return cos, sin


def kernel(x, c, w_ada, b_ada, g_norm, g_final, a_w_in, a_g_v, a_w_s, a_b_s, a_w_out,
           b_w_qkv, b_g_q, b_g_k, b_w_o, f_w_up, f_conv_w, f_conv_b, f_w_down):
    depth = w_ada.shape[0]
    bsz, s, d = x.shape
    hd = b_g_q.shape[-1]
    n_heads = d // hd
    n_kv = (b_w_qkv.shape[-1] // hd - n_heads) // 2
    n_mixers = 2

    mod = _adaln(c, w_ada, b_ada)
    w_up_bf16 = f_w_up.astype(BF16)
    w_down_bf16 = f_w_down.astype(BF16)
    cos, sin = _rope_tables(s, hd)

    for i in range(depth):
        j = i // n_mixers
        if i % n_mixers == 0:
            n_groups, chunk, _ = a_w_s.shape[1:]
            width = a_w_out.shape[1]
            bs_full = jnp.repeat(jnp.transpose(a_b_s[j]), width // n_groups, axis=1)
            u, vn = _gmlp_in(x, mod[i], g_norm[i, 0], a_w_in[j].astype(BF16), a_g_v[j])
            x = _gmlp_out(u, vn, a_w_s[j].astype(BF16), bs_full, x, mod[i], a_w_out[j].astype(BF16))
        else:
            w = b_w_qkv[j]
            qk_cols = (n_heads + n_kv) * hd
            q, k, vt = _qkv(x, mod[i], g_norm[i, 0], w[:, :qk_cols].astype(BF16),
                            jnp.transpose(w[:, qk_cols:]).astype(BF16), b_g_q[j], b_g_k[j],
                            cos, sin, n_kv)
            score_bound = (1.01 * math.log2(math.e) * math.sqrt(hd)
                           * jnp.max(jnp.abs(b_g_q[j])) * jnp.max(jnp.abs(b_g_k[j])))
            a = _attention(q, k, vt, score_bound)
            x = _oproj(a, b_w_o[j].astype(BF16), x, mod[i])
        x = _ffn(x, mod[i], g_norm[i, 1], i, w_up_bf16, f_conv_w, f_conv_b, w_down_bf16,
                 g_final, final_norm=(i == depth - 1))
    return x
```

```python
import functools
import math

import jax
import jax.numpy as jnp
from jax import lax
from jax.experimental import pallas as pl
from jax.experimental.pallas import tpu as pltpu

EPS = 1e-6
GRID_W = 64
ROPE_THETA = 10000.0
V7X_VMEM_BYTES = 64 * 1024 * 1024
V7X_SUBLANES = 8
V7X_BF16_SUBLANES = 16
VMEM_LIMIT_BYTES = V7X_VMEM_BYTES - 8 * 1024 * 1024

BF16 = jnp.bfloat16
F32 = jnp.float32


def _compiler_params(n_grid_axes):
    return pltpu.CompilerParams(
        dimension_semantics=("arbitrary",) * n_grid_axes,
        vmem_limit_bytes=VMEM_LIMIT_BYTES,
    )


def _rms_scale(x):
    return lax.rsqrt(jnp.mean(x * x, axis=-1, keepdims=True) + EPS)


def _modnorm(x, gain, scale, shift):
    return (x * _rms_scale(x)) * (gain * (1.0 + scale)) + shift


def _gelu(x):
    return 0.5 * x * (1.0 + lax.erf(x * (1.0 / math.sqrt(2.0))))


def _adaln_kernel(c_ref, w_ref, b_ref, o_ref):
    c = c_ref[...]
    cond = c * (1.0 / (1.0 + jnp.exp(-c)))
    o_ref[...] = jnp.dot(cond, w_ref[...], precision=lax.Precision.HIGHEST,
                         preferred_element_type=F32) + b_ref[...]


def _adaln(c, w_ada, b_ada, tn=1024):
    depth, d, n = w_ada.shape
    bsz = c.shape[0]
    rows = -(-bsz // V7X_SUBLANES) * V7X_SUBLANES
    c_pad = jnp.zeros((rows, d), F32).at[:bsz].set(c)
    out = pl.pallas_call(
        _adaln_kernel,
        grid=(depth, n // tn),
        in_specs=[
            pl.BlockSpec((rows, d), lambda l, j: (0, 0)),
            pl.BlockSpec((None, d, tn), lambda l, j: (l, 0, j)),
            pl.BlockSpec((None, 1, tn), lambda l, j: (l, 0, j)),
        ],
        out_specs=pl.BlockSpec((None, rows, tn), lambda l, j: (l, 0, j)),
        out_shape=jax.ShapeDtypeStruct((depth, rows, n), F32),
        compiler_params=_compiler_params(2),
        name="adaln",
    )(c_pad, w_ada, b_ada.reshape(depth, 1, n))
    return out[:, :bsz].reshape(depth, bsz, 6, d)


def _gmlp_in_kernel(x_ref, mod_ref, g_ref, w_ref, gv_ref, u_ref, vn_ref, h_scr, v_scr, *, nc):
    width = u_ref.shape[-1]
    h_scr[...] = _modnorm(x_ref[...], g_ref[...], mod_ref[1:2, :], mod_ref[0:1, :]).astype(BF16)
    ssq = jnp.zeros((x_ref.shape[0], 1), F32)
    for c in range(width // nc):
        cols = slice(c * nc, (c + 1) * nc)
        zu = _gelu(jnp.dot(h_scr[...], w_ref[:, cols], preferred_element_type=F32))
        u_ref[:, cols] = zu.astype(BF16)
        zv = _gelu(jnp.dot(h_scr[...], w_ref[:, width + c * nc: width + (c + 1) * nc],
                           preferred_element_type=F32))
        v_scr[:, cols] = zv
        ssq = ssq + jnp.sum(zv * zv, axis=-1, keepdims=True)
    r = lax.rsqrt(ssq * (1.0 / width) + EPS)
    vn_ref[...] = ((v_scr[...] * r) * gv_ref[...]).astype(BF16)


def _gmlp_in(x, mod, g, w_in, g_v, tm=512, nc=512):
    bsz, s, d = x.shape
    width = w_in.shape[1] // 2
    kernel = functools.partial(_gmlp_in_kernel, nc=nc)
    return pl.pallas_call(
        kernel,
        grid=(bsz, s // tm),
        in_specs=[
            pl.BlockSpec((None, tm, d), lambda b, i: (b, i, 0)),
            pl.BlockSpec((None, 6, d), lambda b, i: (b, 0, 0)),
            pl.BlockSpec((1, d), lambda b, i: (0, 0)),
            pl.BlockSpec((d, 2 * width), lambda b, i: (0, 0)),
            pl.BlockSpec((1, width), lambda b, i: (0, 0)),
        ],
        out_specs=[
            pl.BlockSpec((None, tm, width), lambda b, i: (b, i, 0)),
            pl.BlockSpec((None, tm, width), lambda b, i: (b, i, 0)),
        ],
        out_shape=[jax.ShapeDtypeStruct((bsz, s, width), BF16)] * 2,
        scratch_shapes=[pltpu.VMEM((tm, d), BF16), pltpu.VMEM((tm, width), F32)],
        compiler_params=_compiler_params(2),
        name="gmlp_in",
    )(x, mod, g.reshape(1, d), w_in, g_v.reshape(1, width))


def _gmlp_out_kernel(u_ref, vn_ref, ws_ref, bs_ref, x_ref, mod_ref, wo_ref, o_ref, y_scr):
    n_groups, chunk, _ = ws_ref.shape
    gd = u_ref.shape[-1] // n_groups
    for n in range(u_ref.shape[0] // chunk):
        rows = slice(n * chunk, (n + 1) * chunk)
        for g in range(n_groups):
            cols = slice(g * gd, (g + 1) * gd)
            sv = jnp.dot(ws_ref[g], vn_ref[rows, cols], preferred_element_type=F32) + bs_ref[:, cols]
            y_scr[rows, cols] = (u_ref[rows, cols].astype(F32) * sv).astype(BF16)
    y = jnp.dot(y_scr[...], wo_ref[...], preferred_element_type=F32)
    o_ref[...] = x_ref[...] + mod_ref[2:3, :] * y


def _gmlp_out(u, vn, w_s, bs_full, x, mod, w_out, tm=512):
    bsz, s, d = x.shape
    width = u.shape[-1]
    n_groups, chunk, _ = w_s.shape
    return pl.pallas_call(
        _gmlp_out_kernel,
        grid=(bsz, s // tm),
        in_specs=[
            pl.BlockSpec((None, tm, width), lambda b, i: (b, i, 0)),
            pl.BlockSpec((None, tm, width), lambda b, i: (b, i, 0)),
            pl.BlockSpec((n_groups, chunk, chunk), lambda b, i: (0, 0, 0)),
            pl.BlockSpec((chunk, width), lambda b, i: (0, 0)),
            pl.BlockSpec((None, tm, d), lambda b, i: (b, i, 0)),
            pl.BlockSpec((None, 6, d), lambda b, i: (b, 0, 0)),
            pl.BlockSpec((width, d), lambda b, i: (0, 0)),
        ],
        out_specs=pl.BlockSpec((None, tm, d), lambda b, i: (b, i, 0)),
        out_shape=jax.ShapeDtypeStruct((bsz, s, d), F32),
        scratch_shapes=[pltpu.VMEM((tm, width), BF16)],
        compiler_params=_compiler_params(2),
        name="gmlp_out",
    )(u, vn, w_s, bs_full, x, mod, w_out)


def _rope(t, cos, sin_signed, half_idx):
    quarter = t.shape[-1] // 4
    partner = jnp.where(half_idx, pltpu.roll(t, t.shape[-1] - quarter, 1), pltpu.roll(t, quarter, 1))
    return t * cos + partner * sin_signed


def _qkv_kernel(x_ref, mod_ref, g_ref, wqk_ref, wvt_ref, gq_ref, gk_ref, cos_ref, sin_ref,
                q_ref, k_ref, vt_ref, *, q_scale, row_chunk):
    n_kv, group, tm, hd = q_ref.shape
    lane = lax.broadcasted_iota(jnp.int32, (row_chunk, hd), 1)
    half_idx = (lane % (hd // 2)) < (hd // 4)
    pad_rows = vt_ref.shape[2] - hd
    ones_row = (lax.broadcasted_iota(jnp.int32, (pad_rows, row_chunk), 0) == 0).astype(BF16)

    for r0 in range(0, tm, row_chunk):
        rows = slice(r0, r0 + row_chunk)
        h = _modnorm(x_ref[rows, :], g_ref[...], mod_ref[1:2, :], mod_ref[0:1, :]).astype(BF16)
        cos = cos_ref[rows, :]
        sin = sin_ref[rows, :]

        def head(t, gain, post):
            t = t * lax.rsqrt(jnp.mean(t * t, axis=-1, keepdims=True) + EPS) * gain
            return (_rope(t, cos, sin, half_idx) * post).astype(BF16)

        for kv in range(n_kv):
            cols = slice(kv * group * hd, (kv + 1) * group * hd)
            tq = jnp.dot(h, wqk_ref[:, cols], preferred_element_type=F32)
            for g in range(group):
                q_ref[kv, g, rows, :] = head(tq[:, g * hd:(g + 1) * hd], gq_ref[...], q_scale)
        k_off = n_kv * group * hd
        tk = jnp.dot(h, wqk_ref[:, k_off:k_off + n_kv * hd], preferred_element_type=F32)
        for kv in range(n_kv):
            k_ref[kv, rows, :] = head(tk[:, kv * hd:(kv + 1) * hd], gk_ref[...], 1.0)
        vt = lax.dot_general(wvt_ref[...], h, (((1,), (1,)), ((), ())),
                             preferred_element_type=F32)
        for kv in range(n_kv):
            vt_ref[kv, 0, :hd, rows] = vt[kv * hd:(kv + 1) * hd, :].astype(BF16)
            vt_ref[kv, 0, hd:, rows] = ones_row


def _qkv(x, mod, g, w_qk, w_vt, g_q, g_k, cos, sin, n_kv, tm=512, row_chunk=256):
    bsz, s, d = x.shape
    hd = g_q.shape[-1]
    n_heads = d // hd
    group = n_heads // n_kv
    kernel = functools.partial(_qkv_kernel, q_scale=math.log2(math.e) / math.sqrt(hd),
                               row_chunk=row_chunk)
    return pl.pallas_call(
        kernel,
        grid=(bsz, s // tm),
        in_specs=[
            pl.BlockSpec((None, tm, d), lambda b, i: (b, i, 0)),
            pl.BlockSpec((None, 6, d), lambda b, i: (b, 0, 0)),
            pl.BlockSpec((1, d), lambda b, i: (0, 0)),
            pl.BlockSpec(w_qk.shape, lambda b, i: (0, 0)),
            pl.BlockSpec(w_vt.shape, lambda b, i: (0, 0)),
            pl.BlockSpec((1, hd), lambda b, i: (0, 0)),
            pl.BlockSpec((1, hd), lambda b, i: (0, 0)),
            pl.BlockSpec((tm, hd), lambda b, i: (i, 0)),
            pl.BlockSpec((tm, hd), lambda b, i: (i, 0)),
        ],
        out_specs=[
            pl.BlockSpec((None, n_kv, group, tm, hd), lambda b, i: (b, 0, 0, i, 0)),
            pl.BlockSpec((None, n_kv, tm, hd), lambda b, i: (b, 0, i, 0)),
            pl.BlockSpec((None, n_kv, 1, hd + V7X_BF16_SUBLANES, tm), lambda b, i: (b, 0, i, 0, 0)),
        ],
        out_shape=[
            jax.ShapeDtypeStruct((bsz, n_kv, group, s, hd), BF16),
            jax.ShapeDtypeStruct((bsz, n_kv, s, hd), BF16),
            jax.ShapeDtypeStruct((bsz, n_kv, s // tm, hd + V7X_BF16_SUBLANES, tm), BF16),
        ],
        compiler_params=_compiler_params(2),
        name="qkv",
    )(x, mod, g.reshape(1, d), w_qk, w_vt, g_q.reshape(1, hd), g_k.reshape(1, hd), cos, sin)


ATTN_SLOTS = 4


def _attn_kernel(q_ref, k_ref, vt_ref, o_ref, acc_scr, *slots):
    group, tq, hd = q_ref.shape
    n_kt, _, tk = vt_ref.shape
    nq = group * tq
    s_scr, p_scr = slots[:ATTN_SLOTS], slots[ATTN_SLOTS:]
    st_idx = jnp.minimum(pl.program_id(2), 0)
    ld_idx = jnp.minimum(pl.program_id(1), 0)

    def scores(kt, s_scr):
        k_tile = k_ref[pl.ds(pl.multiple_of(kt * tk, tk), tk), :]
        s = lax.dot_general(k_tile, q_ref[...].reshape(nq, hd), (((1,), (1,)), ((), ())),
                            preferred_element_type=F32)
        s_scr[st_idx] = s
        return jnp.max(s, axis=0, keepdims=True)

    def softmax(s_scr, p_scr, m, tile_max):
        m_new = jnp.maximum(m, tile_max)
        p_scr[st_idx] = jnp.exp2(s_scr[ld_idx] - m_new).astype(BF16)
        return m_new, jnp.exp2(m - m_new)

    def values(kt, p_scr, alpha):
        pv = jnp.dot(vt_ref[kt], p_scr[ld_idx], preferred_element_type=F32)
        acc_scr[...] = alpha * acc_scr[...] + pv

    def tile(t, slot, carry, prefetch):
        m, max_t, max_t1, alpha_t2, alpha_t1 = carry
        max_t2 = scores(t + 2, s_scr[(slot + 2) % ATTN_SLOTS]) if prefetch else None
        if alpha_t2 is not None:
            values(t - 2, p_scr[(slot - 2) % ATTN_SLOTS], alpha_t2)
        m, alpha = softmax(s_scr[slot], p_scr[slot], m, max_t)
        return m, max_t1, max_t2, alpha_t1, alpha

    def group_of_tiles(t0, carry, n_prefetch=ATTN_SLOTS):
        for slot in range(ATTN_SLOTS):
            carry = tile(t0 + slot, slot, carry, prefetch=slot < n_prefetch)
        return carry

    n_groups = n_kt // ATTN_SLOTS
    acc_scr[...] = jnp.zeros_like(acc_scr)
    m0 = jnp.full((1, nq), -jnp.inf, F32)
    carry = (m0, scores(0, s_scr[0]), scores(1, s_scr[1]), None, None)
    carry = group_of_tiles(0, carry)
    carry = lax.fori_loop(1, n_groups - 1, lambda g, c: group_of_tiles(g * ATTN_SLOTS, c), carry,
                          unroll=True)
    _, _, _, alpha_t2, alpha_t1 = group_of_tiles(n_kt - ATTN_SLOTS, carry,
                                                 n_prefetch=ATTN_SLOTS - 2)
    values(n_kt - 2, p_scr[ATTN_SLOTS - 2], alpha_t2)
    values(n_kt - 1, p_scr[ATTN_SLOTS - 1], alpha_t1)
    out_t = acc_scr[:hd, :] / acc_scr[hd:hd + 1, :]
    for g in range(group):
        o_ref[:, g * hd:(g + 1) * hd] = out_t[:, g * tq:(g + 1) * tq].T.astype(BF16)


def _attn_bounded_kernel(q_ref, k_ref, vt_ref, o_ref, acc_scr, *p_scr):
    group, tq, hd = q_ref.shape
    n_kt, _, tk = vt_ref.shape
    nq = group * tq
    st_idx = jnp.minimum(pl.program_id(2), 0)
    ld_idx = jnp.minimum(pl.program_id(1), 0)

    def probabilities(kt):
        k_tile = k_ref[kt * tk:(kt + 1) * tk, :]
        s = lax.dot_general(k_tile, q_ref[...].reshape(nq, hd), (((1,), (1,)), ((), ())),
                            preferred_element_type=F32)
        p = jnp.exp2(s)
        p_scr[kt % ATTN_SLOTS][st_idx] = p.astype(BF16)
        return jnp.sum(p, axis=0, keepdims=True)

    def values(kt):
        acc_scr[...] += jnp.dot(vt_ref[kt, :hd, :], p_scr[kt % ATTN_SLOTS][ld_idx],
                                preferred_element_type=F32)

    acc_scr[...] = jnp.zeros_like(acc_scr)
    l = probabilities(0) + probabilities(1)
    for kt in range(n_kt):
        if kt + 2 < n_kt:
            l = l + probabilities(kt + 2)
        values(kt)
    out_t = acc_scr[...] / l
    for g in range(group):
        o_ref[:, g * hd:(g + 1) * hd] = out_t[:, g * tq:(g + 1) * tq].T.astype(BF16)


MAX_UNSTABILISED_SCORE = 60.0


def _attention(q, k, vt, score_bound, tq=128):
    bsz, n_kv, group, s, hd = q.shape
    n_kt, tk = vt.shape[2], vt.shape[4]
    assert n_kt % ATTN_SLOTS == 0 and n_kt >= 2 * ATTN_SLOTS
    nq = group * tq

    def call(body, scratch, name):
        return pl.pallas_call(
            body,
            grid=(bsz, n_kv, s // tq),
            in_specs=[
                pl.BlockSpec((None, None, group, tq, hd), lambda b, h, i: (b, h, 0, i, 0)),
                pl.BlockSpec((None, None, s, hd), lambda b, h, i: (b, h, 0, 0)),
                pl.BlockSpec((None, None, n_kt, vt.shape[3], tk), lambda b, h, i: (b, h, 0, 0, 0)),
            ],
            out_specs=pl.BlockSpec((None, tq, group * hd), lambda b, h, i: (b, i, h)),
            out_shape=jax.ShapeDtypeStruct((bsz, s, n_kv * group * hd), BF16),
            scratch_shapes=scratch,
            compiler_params=_compiler_params(3),
            name=name,
        )(q, k, vt)

    p_slots = [pltpu.VMEM((1, tk, nq), BF16)] * ATTN_SLOTS
    s_slots = [pltpu.VMEM((1, tk, nq), F32)] * ATTN_SLOTS
    return lax.cond(
        score_bound <= MAX_UNSTABILISED_SCORE,
        lambda: call(_attn_bounded_kernel, [pltpu.VMEM((hd, nq), F32)] + p_slots,
                     "attention_bounded"),
        lambda: call(_attn_kernel, [pltpu.VMEM((vt.shape[3], nq), F32)] + s_slots + p_slots,
                     "attention"),
    )


def _oproj_kernel(a_ref, w_ref, x_ref, mod_ref, o_ref):
    y = jnp.dot(a_ref[...], w_ref[...], preferred_element_type=F32)
    o_ref[...] = x_ref[...] + mod_ref[2:3, :] * y


def _oproj(a, w_o, x, mod, tm=512):
    bsz, s, d = x.shape
    return pl.pallas_call(
        _oproj_kernel,
        grid=(bsz, s // tm),
        in_specs=[
            pl.BlockSpec((None, tm, a.shape[-1]), lambda b, i: (b, i, 0)),
            pl.BlockSpec(w_o.shape, lambda b, i: (0, 0)),
            pl.BlockSpec((None, tm, d), lambda b, i: (b, i, 0)),
            pl.BlockSpec((None, 6, d), lambda b, i: (b, 0, 0)),
        ],
        out_specs=pl.BlockSpec((None, tm, d), lambda b, i: (b, i, 0)),
        out_shape=jax.ShapeDtypeStruct((bsz, s, d), F32),
        compiler_params=_compiler_params(2),
        name="oproj",
    )(a, w_o, x, mod)


def _ffn_kernel(x_ref, xp_ref, xn_ref, mod_ref, g_ref, wg_ref, wv_ref, cwg_ref, cwv_ref,
                cbg_ref, cbv_ref, wd_ref, gf_ref, o_ref, h_scr, ag_scr, av_scr, *, final_norm):
    tm = x_ref.shape[0]
    halo = xp_ref.shape[0]
    i = pl.program_id(1)
    j = pl.program_id(2)

    @pl.when(j == 0)
    def _():
        gain, scale, shift = g_ref[...], mod_ref[4:5, :], mod_ref[3:4, :]
        h_scr[halo:halo + tm, :] = _modnorm(x_ref[...], gain, scale, shift).astype(BF16)
        hp = _modnorm(xp_ref[...], gain, scale, shift)
        hn = _modnorm(xn_ref[...], gain, scale, shift)
        h_scr[0:halo, :] = jnp.where(i == 0, 0.0, hp).astype(BF16)
        h_scr[halo + tm:, :] = jnp.where(i == pl.num_programs(1) - 1, 0.0, hn).astype(BF16)
        o_ref[...] = jnp.zeros_like(o_ref)

    def conv(a_scr, cw_ref, cb_ref):
        rc = a_scr.shape[0] - 2 * halo
        return (cw_ref[0:1, :] * a_scr[halo - 1:halo - 1 + rc, :]
                + cw_ref[1:2, :] * a_scr[halo:halo + rc, :]
                + cw_ref[2:3, :] * a_scr[halo + 1:halo + 1 + rc, :]
                + cb_ref[...])

    n_chunks = ag_scr.shape[0]
    rc = tm // n_chunks
    for c in range(n_chunks):
        h = h_scr[c * rc:c * rc + rc + 2 * halo, :]
        ag_scr[c] = jnp.dot(h, wg_ref[...], preferred_element_type=F32)
        av_scr[c] = jnp.dot(h, wv_ref[...], preferred_element_type=F32)
    for c in range(n_chunks):
        act = (_gelu(conv(ag_scr.at[c], cwg_ref, cbg_ref))
               * conv(av_scr.at[c], cwv_ref, cbv_ref)).astype(BF16)
        o_ref[c * rc:(c + 1) * rc, :] += jnp.dot(act, wd_ref[...], preferred_element_type=F32)

    @pl.when(j == pl.num_programs(2) - 1)
    def _():
        y = x_ref[...] + mod_ref[5:6, :] * o_ref[...]
        if final_norm:
            y = (y * _rms_scale(y)) * gf_ref[...]
        o_ref[...] = y


def _ffn(x, mod, g, layer, w_up, conv_w, conv_b, w_down, g_final, final_norm,
         tm=512, tf=512, row_chunks=2):
    bsz, s, d = x.shape
    dff = w_down.shape[1]
    nj = dff // tf
    halo = V7X_SUBLANES
    nhb = tm // halo
    last_hb = s // halo - 1
    kernel = functools.partial(_ffn_kernel, final_norm=final_norm)
    return pl.pallas_call(
        kernel,
        grid=(bsz, s // tm, nj),
        in_specs=[
            pl.BlockSpec((None, tm, d), lambda b, i, j: (b, i, 0)),
            pl.BlockSpec((None, halo, d), lambda b, i, j: (b, jnp.maximum(i * nhb - 1, 0), 0)),
            pl.BlockSpec((None, halo, d), lambda b, i, j: (b, jnp.minimum((i + 1) * nhb, last_hb), 0)),
            pl.BlockSpec((None, 6, d), lambda b, i, j: (b, 0, 0)),
            pl.BlockSpec((1, d), lambda b, i, j: (0, 0)),
            pl.BlockSpec((None, d, tf), lambda b, i, j: (layer, 0, j)),
            pl.BlockSpec((None, d, tf), lambda b, i, j: (layer, 0, nj + j)),
            pl.BlockSpec((None, 3, tf), lambda b, i, j: (layer, 0, j)),
            pl.BlockSpec((None, 3, tf), lambda b, i, j: (layer, 0, nj + j)),
            pl.BlockSpec((None, 1, tf), lambda b, i, j: (layer, 0, j)),
            pl.BlockSpec((None, 1, tf), lambda b, i, j: (layer, 0, nj + j)),
            pl.BlockSpec((None, tf, d), lambda b, i, j: (layer, j, 0)),
            pl.BlockSpec((1, d), lambda b, i, j: (0, 0)),
        ],
        out_specs=pl.BlockSpec((None, tm, d), lambda b, i, j: (b, i, 0)),
        out_shape=jax.ShapeDtypeStruct((bsz, s, d), F32),
        scratch_shapes=[
            pltpu.VMEM((tm + 2 * halo, d), BF16),
            pltpu.VMEM((row_chunks, tm // row_chunks + 2 * halo, tf), F32),
            pltpu.VMEM((row_chunks, tm // row_chunks + 2 * halo, tf), F32),
        ],
        compiler_params=_compiler_params(3),
        name="ffn",
    )(x, x, x, mod, g.reshape(1, d), w_up, w_up, conv_w, conv_w,
      conv_b[:, None, :], conv_b[:, None, :], w_down, g_final.reshape(1, d))


def _rope_tables(s, hd):
    axis_dim = hd // 2
    n_rows = s // GRID_W
    inv_freq = ROPE_THETA ** (-jnp.arange(0, axis_dim, 2, dtype=F32) / axis_dim)
    ang_r = jnp.arange(n_rows).astype(F32)[:, None] * inv_freq[None, :]
    ang_c = jnp.arange(GRID_W).astype(F32)[:, None] * inv_freq[None, :]
    by_row = lambda t: jnp.repeat(t, GRID_W, axis=0)
    by_col = lambda t: jnp.tile(t, (n_rows, 1))
    cos_r, sin_r = by_row(jnp.cos(ang_r)), by_row(jnp.sin(ang_r))
    cos_c, sin_c = by_col(jnp.cos(ang_c)), by_col(jnp.sin(ang_c))
    cos = jnp.concatenate([cos_r, cos_r, cos_c, cos_c], axis=-1)
    sin = jnp.concatenate([-sin_r, sin_r, -sin_c, sin_c], axis=-1)
    return cos, sin


def kernel(x, c, w_ada, b_ada, g_norm, g_final, a_w_in, a_g_v, a_w_s, a_b_s, a_w_out,
           b_w_qkv, b_g_q, b_g_k, b_w_o, f_w_up, f_conv_w, f_conv_b, f_w_down):
    depth = w_ada.shape[0]
    bsz, s, d = x.shape
    hd = b_g_q.shape[-1]
    n_heads = d // hd
    n_kv = (b_w_qkv.shape[-1] // hd - n_heads) // 2
    n_mixers = 2

    mod = _adaln(c, w_ada, b_ada)
    w_up_bf16 = f_w_up.astype(BF16)
    w_down_bf16 = f_w_down.astype(BF16)
    cos, sin = _rope_tables(s, hd)

    for i in range(depth):
        j = i // n_mixers
        if i % n_mixers == 0:
            n_groups, chunk, _ = a_w_s.shape[1:]
            width = a_w_out.shape[1]
            bs_full = jnp.repeat(jnp.transpose(a_b_s[j]), width // n_groups, axis=1)
            u, vn = _gmlp_in(x, mod[i], g_norm[i, 0], a_w_in[j].astype(BF16), a_g_v[j])
            x = _gmlp_out(u, vn, a_w_s[j].astype(BF16), bs_full, x, mod[i], a_w_out[j].astype(BF16))
        else:
            w = b_w_qkv[j]
            qk_cols = (n_heads + n_kv) * hd
            q, k, vt = _qkv(x, mod[i], g_norm[i, 0], w[:, :qk_cols].astype(BF16),
                            jnp.transpose(w[:, qk_cols:]).astype(BF16), b_g_q[j], b_g_k[j],
                            cos, sin, n_kv)
            score_bound = (1.01 * math.log2(math.e) * math.sqrt(hd)
                           * jnp.max(jnp.abs(b_g_q[j])) * jnp.max(jnp.abs(b_g_k[j])))
            a = _attention(q, k, vt, score_bound)
            x = _oproj(a, b_w_o[j].astype(BF16), x, mod[i])
        x = _ffn(x, mod[i], g_norm[i, 1], i, w_up_bf16, f_conv_w, f_conv_b, w_down_bf16,
                 g_final, final_norm=(i == depth - 1))
    return x
```

```python
import functools
import math

import jax
import jax.numpy as jnp
from jax import lax
from jax.experimental import pallas as pl
from jax.experimental.pallas import tpu as pltpu

EPS = 1e-6
GRID_W = 64
ROPE_THETA = 10000.0
V7X_VMEM_BYTES = 64 * 1024 * 1024
V7X_SUBLANES = 8
V7X_BF16_SUBLANES = 16
VMEM_LIMIT_BYTES = V7X_VMEM_BYTES - 8 * 1024 * 1024

BF16 = jnp.bfloat16
F32 = jnp.float32


def _compiler_params(n_grid_axes):
    return pltpu.CompilerParams(
        dimension_semantics=("arbitrary",) * n_grid_axes,
        vmem_limit_bytes=VMEM_LIMIT_BYTES,
    )


def _rms_scale(x):
    return lax.rsqrt(jnp.mean(x * x, axis=-1, keepdims=True) + EPS)


def _modnorm(x, gain, scale, shift):
    return (x * _rms_scale(x)) * (gain * (1.0 + scale)) + shift


def _gelu(x):
    return 0.5 * x * (1.0 + lax.erf(x * (1.0 / math.sqrt(2.0))))


def _adaln_kernel(c_ref, w_ref, b_ref, o_ref):
    c = c_ref[...]
    cond = c * (1.0 / (1.0 + jnp.exp(-c)))
    o_ref[...] = jnp.dot(cond, w_ref[...], precision=lax.Precision.HIGHEST,
                         preferred_element_type=F32) + b_ref[...]


def _adaln(c, w_ada, b_ada, tn=1024):
    depth, d, n = w_ada.shape
    bsz = c.shape[0]
    rows = -(-bsz // V7X_SUBLANES) * V7X_SUBLANES
    c_pad = jnp.zeros((rows, d), F32).at[:bsz].set(c)
    out = pl.pallas_call(
        _adaln_kernel,
        grid=(depth, n // tn),
        in_specs=[
            pl.BlockSpec((rows, d), lambda l, j: (0, 0)),
            pl.BlockSpec((None, d, tn), lambda l, j: (l, 0, j)),
            pl.BlockSpec((None, 1, tn), lambda l, j: (l, 0, j)),
        ],
        out_specs=pl.BlockSpec((None, rows, tn), lambda l, j: (l, 0, j)),
        out_shape=jax.ShapeDtypeStruct((depth, rows, n), F32),
        compiler_params=_compiler_params(2),
        name="adaln",
    )(c_pad, w_ada, b_ada.reshape(depth, 1, n))
    return out[:, :bsz].reshape(depth, bsz, 6, d)


def _gmlp_in_kernel(x_ref, mod_ref, g_ref, w_ref, gv_ref, u_ref, vn_ref, h_scr, v_scr, *, nc):
    width = u_ref.shape[-1]
    h_scr[...] = _modnorm(x_ref[...], g_ref[...], mod_ref[1:2, :], mod_ref[0:1, :]).astype(BF16)
    ssq = jnp.zeros((x_ref.shape[0], 1), F32)
    for c in range(width // nc):
        cols = slice(c * nc, (c + 1) * nc)
        zu = _gelu(jnp.dot(h_scr[...], w_ref[:, cols], preferred_element_type=F32))
        u_ref[:, cols] = zu.astype(BF16)
        zv = _gelu(jnp.dot(h_scr[...], w_ref[:, width + c * nc: width + (c + 1) * nc],
                           preferred_element_type=F32))
        v_scr[:, cols] = zv
        ssq = ssq + jnp.sum(zv * zv, axis=-1, keepdims=True)
    r = lax.rsqrt(ssq * (1.0 / width) + EPS)
    vn_ref[...] = ((v_scr[...] * r) * gv_ref[...]).astype(BF16)


def _gmlp_in(x, mod, g, w_in, g_v, tm=512, nc=512):
    bsz, s, d = x.shape
    width = w_in.shape[1] // 2
    kernel = functools.partial(_gmlp_in_kernel, nc=nc)
    return pl.pallas_call(
        kernel,
        grid=(bsz, s // tm),
        in_specs=[
            pl.BlockSpec((None, tm, d), lambda b, i: (b, i, 0)),
            pl.BlockSpec((None, 6, d), lambda b, i: (b, 0, 0)),
            pl.BlockSpec((1, d), lambda b, i: (0, 0)),
            pl.BlockSpec((d, 2 * width), lambda b, i: (0, 0)),
            pl.BlockSpec((1, width), lambda b, i: (0, 0)),
        ],
        out_specs=[
            pl.BlockSpec((None, tm, width), lambda b, i: (b, i, 0)),
            pl.BlockSpec((None, tm, width), lambda b, i: (b, i, 0)),
        ],
        out_shape=[jax.ShapeDtypeStruct((bsz, s, width), BF16)] * 2,
        scratch_shapes=[pltpu.VMEM((tm, d), BF16), pltpu.VMEM((tm, width), F32)],
        compiler_params=_compiler_params(2),
        name="gmlp_in",
    )(x, mod, g.reshape(1, d), w_in, g_v.reshape(1, width))


def _gmlp_out_kernel(u_ref, vn_ref, ws_ref, bs_ref, x_ref, mod_ref, wo_ref, o_ref, y_scr):
    n_groups, chunk, _ = ws_ref.shape
    gd = u_ref.shape[-1] // n_groups
    for n in range(u_ref.shape[0] // chunk):
        rows = slice(n * chunk, (n + 1) * chunk)
        for g in range(n_groups):
            cols = slice(g * gd, (g + 1) * gd)
            sv = jnp.dot(ws_ref[g], vn_ref[rows, cols], preferred_element_type=F32) + bs_ref[:, cols]
            y_scr[rows, cols] = (u_ref[rows, cols].astype(F32) * sv).astype(BF16)
    y = jnp.dot(y_scr[...], wo_ref[...], preferred_element_type=F32)
    o_ref[...] = x_ref[...] + mod_ref[2:3, :] * y


def _gmlp_out(u, vn, w_s, bs_full, x, mod, w_out, tm=512):
    bsz, s, d = x.shape
    width = u.shape[-1]
    n_groups, chunk, _ = w_s.shape
    return pl.pallas_call(
        _gmlp_out_kernel,
        grid=(bsz, s // tm),
        in_specs=[
            pl.BlockSpec((None, tm, width), lambda b, i: (b, i, 0)),
            pl.BlockSpec((None, tm, width), lambda b, i: (b, i, 0)),
            pl.BlockSpec((n_groups, chunk, chunk), lambda b, i: (0, 0, 0)),
            pl.BlockSpec((chunk, width), lambda b, i: (0, 0)),
            pl.BlockSpec((None, tm, d), lambda b, i: (b, i, 0)),
            pl.BlockSpec((None, 6, d), lambda b, i: (b, 0, 0)),
            pl.BlockSpec((width, d), lambda b, i: (0, 0)),
        ],
        out_specs=pl.BlockSpec((None, tm, d), lambda b, i: (b, i, 0)),
        out_shape=jax.ShapeDtypeStruct((bsz, s, d), F32),
        scratch_shapes=[pltpu.VMEM((tm, width), BF16)],
        compiler_params=_compiler_params(2),
        name="gmlp_out",
    )(u, vn, w_s, bs_full, x, mod, w_out)


def _rope(t, cos, sin_signed, half_idx):
    quarter = t.shape[-1] // 4
    partner = jnp.where(half_idx, pltpu.roll(t, t.shape[-1] - quarter, 1), pltpu.roll(t, quarter, 1))
    return t * cos + partner * sin_signed


def _qkv_kernel(x_ref, mod_ref, g_ref, wqk_ref, wvt_ref, gq_ref, gk_ref, cos_ref, sin_ref,
                q_ref, k_ref, vt_ref, *, q_scale, row_chunk):
    n_kv, group, tm, hd = q_ref.shape
    lane = lax.broadcasted_iota(jnp.int32, (row_chunk, hd), 1)
    half_idx = (lane % (hd // 2)) < (hd // 4)
    pad_rows = vt_ref.shape[2] - hd
    ones_row = (lax.broadcasted_iota(jnp.int32, (pad_rows, row_chunk), 0) == 0).astype(BF16)

    for r0 in range(0, tm, row_chunk):
        rows = slice(r0, r0 + row_chunk)
        h = _modnorm(x_ref[rows, :], g_ref[...], mod_ref[1:2, :], mod_ref[0:1, :]).astype(BF16)
        cos = cos_ref[rows, :]
        sin = sin_ref[rows, :]

        def head(t, gain, post):
            t = t * lax.rsqrt(jnp.mean(t * t, axis=-1, keepdims=True) + EPS) * gain
            return (_rope(t, cos, sin, half_idx) * post).astype(BF16)

        for kv in range(n_kv):
            cols = slice(kv * group * hd, (kv + 1) * group * hd)
            tq = jnp.dot(h, wqk_ref[:, cols], preferred_element_type=F32)
            for g in range(group):
                q_ref[kv, g, rows, :] = head(tq[:, g * hd:(g + 1) * hd], gq_ref[...], q_scale)
        k_off = n_kv * group * hd
        tk = jnp.dot(h, wqk_ref[:, k_off:k_off + n_kv * hd], preferred_element_type=F32)
        for kv in range(n_kv):
            k_ref[kv, rows, :] = head(tk[:, kv * hd:(kv + 1) * hd], gk_ref[...], 1.0)
        vt = lax.dot_general(wvt_ref[...], h, (((1,), (1,)), ((), ())),
                             preferred_element_type=F32)
        for kv in range(n_kv):
            vt_ref[kv, 0, :hd, rows] = vt[kv * hd:(kv + 1) * hd, :].astype(BF16)
            vt_ref[kv, 0, hd:, rows] = ones_row


def _qkv(x, mod, g, w_qk, w_vt, g_q, g_k, cos, sin, n_kv, tm=512, row_chunk=256):
    bsz, s, d = x.shape
    hd = g_q.shape[-1]
    n_heads = d // hd
    group = n_heads // n_kv
    kernel = functools.partial(_qkv_kernel, q_scale=math.log2(math.e) / math.sqrt(hd),
                               row_chunk=row_chunk)
    return pl.pallas_call(
        kernel,
        grid=(bsz, s // tm),
        in_specs=[
            pl.BlockSpec((None, tm, d), lambda b, i: (b, i, 0)),
            pl.BlockSpec((None, 6, d), lambda b, i: (b, 0, 0)),
            pl.BlockSpec((1, d), lambda b, i: (0, 0)),
            pl.BlockSpec(w_qk.shape, lambda b, i: (0, 0)),
            pl.BlockSpec(w_vt.shape, lambda b, i: (0, 0)),
            pl.BlockSpec((1, hd), lambda b, i: (0, 0)),
            pl.BlockSpec((1, hd), lambda b, i: (0, 0)),
            pl.BlockSpec((tm, hd), lambda b, i: (i, 0)),
            pl.BlockSpec((tm, hd), lambda b, i: (i, 0)),
        ],
        out_specs=[
            pl.BlockSpec((None, n_kv, group, tm, hd), lambda b, i: (b, 0, 0, i, 0)),
            pl.BlockSpec((None, n_kv, tm, hd), lambda b, i: (b, 0, i, 0)),
            pl.BlockSpec((None, n_kv, 1, hd + V7X_BF16_SUBLANES, tm), lambda b, i: (b, 0, i, 0, 0)),
        ],
        out_shape=[
            jax.ShapeDtypeStruct((bsz, n_kv, group, s, hd), BF16),
            jax.ShapeDtypeStruct((bsz, n_kv, s, hd), BF16),
            jax.ShapeDtypeStruct((bsz, n_kv, s // tm, hd + V7X_BF16_SUBLANES, tm), BF16),
        ],
        compiler_params=_compiler_params(2),
        name="qkv",
    )(x, mod, g.reshape(1, d), w_qk, w_vt, g_q.reshape(1, hd), g_k.reshape(1, hd), cos, sin)


ATTN_SLOTS = 4


def _attn_kernel(q_ref, k_ref, vt_ref, o_ref, acc_scr, *slots):
    group, tq, hd = q_ref.shape
    n_kt, _, tk = vt_ref.shape
    nq = group * tq
    s_scr, p_scr = slots[:ATTN_SLOTS], slots[ATTN_SLOTS:]
    st_idx = jnp.minimum(pl.program_id(2), 0)
    ld_idx = jnp.minimum(pl.program_id(1), 0)

    def scores(kt, s_scr):
        k_tile = k_ref[pl.ds(pl.multiple_of(kt * tk, tk), tk), :]
        s = lax.dot_general(k_tile, q_ref[...].reshape(nq, hd), (((1,), (1,)), ((), ())),
                            preferred_element_type=F32)
        s_scr[st_idx] = s
        return jnp.max(s, axis=0, keepdims=True)

    def softmax(s_scr, p_scr, m, tile_max):
        m_new = jnp.maximum(m, tile_max)
        p_scr[st_idx] = jnp.exp2(s_scr[ld_idx] - m_new).astype(BF16)
        return m_new, jnp.exp2(m - m_new)

    def values(kt, p_scr, alpha):
        pv = jnp.dot(vt_ref[kt], p_scr[ld_idx], preferred_element_type=F32)
        acc_scr[...] = alpha * acc_scr[...] + pv

    def tile(t, slot, carry, prefetch):
        m, max_t, max_t1, alpha_t2, alpha_t1 = carry
        max_t2 = scores(t + 2, s_scr[(slot + 2) % ATTN_SLOTS]) if prefetch else None
        if alpha_t2 is not None:
            values(t - 2, p_scr[(slot - 2) % ATTN_SLOTS], alpha_t2)
        m, alpha = softmax(s_scr[slot], p_scr[slot], m, max_t)
        return m, max_t1, max_t2, alpha_t1, alpha

    def group_of_tiles(t0, carry, n_prefetch=ATTN_SLOTS):
        for slot in range(ATTN_SLOTS):
            carry = tile(t0 + slot, slot, carry, prefetch=slot < n_prefetch)
        return carry

    n_groups = n_kt // ATTN_SLOTS
    acc_scr[...] = jnp.zeros_like(acc_scr)
    m0 = jnp.full((1, nq), -jnp.inf, F32)
    carry = (m0, scores(0, s_scr[0]), scores(1, s_scr[1]), None, None)
    carry = group_of_tiles(0, carry)
    carry = lax.fori_loop(1, n_groups - 1, lambda g, c: group_of_tiles(g * ATTN_SLOTS, c), carry,
                          unroll=True)
    _, _, _, alpha_t2, alpha_t1 = group_of_tiles(n_kt - ATTN_SLOTS, carry,
                                                 n_prefetch=ATTN_SLOTS - 2)
    values(n_kt - 2, p_scr[ATTN_SLOTS - 2], alpha_t2)
    values(n_kt - 1, p_scr[ATTN_SLOTS - 1], alpha_t1)
    out_t = acc_scr[:hd, :] / acc_scr[hd:hd + 1, :]
    for g in range(group):
        o_ref[:, g * hd:(g + 1) * hd] = out_t[:, g * tq:(g + 1) * tq].T.astype(BF16)


def _attn_bounded_kernel(q_ref, k_ref, vt_ref, o_ref, acc_scr, *p_scr):
    group, tq_step, hd = q_ref.shape
    n_kt, _, tk = vt_ref.shape
    n_blocks = acc_scr.shape[0]
    tq = tq_step // n_blocks
    nq = group * tq
    st_idx = jnp.minimum(pl.program_id(2), 0)
    ld_idx = jnp.minimum(pl.program_id(1), 0)

    for blk in range(n_blocks):
        q_rows = slice(blk * tq, (blk + 1) * tq)
        acc = acc_scr.at[blk]

        def probabilities(kt):
            k_tile = k_ref[kt * tk:(kt + 1) * tk, :]
            s = lax.dot_general(k_tile, q_ref[:, q_rows, :].reshape(nq, hd),
                                (((1,), (1,)), ((), ())), preferred_element_type=F32)
            p = jnp.exp2(s)
            p_scr[kt % ATTN_SLOTS][st_idx] = p.astype(BF16)
            return jnp.sum(p, axis=0, keepdims=True)

        def values(kt):
            acc[...] += jnp.dot(vt_ref[kt, :hd, :], p_scr[kt % ATTN_SLOTS][ld_idx],
                                preferred_element_type=F32)

        acc[...] = jnp.zeros_like(acc)
        l = probabilities(0) + probabilities(1)
        for kt in range(n_kt):
            if kt + 2 < n_kt:
                l = l + probabilities(kt + 2)
            values(kt)
        out_t = acc[...] / l
        for g in range(group):
            o_ref[q_rows, g * hd:(g + 1) * hd] = out_t[:, g * tq:(g + 1) * tq].T.astype(BF16)


MAX_UNSTABILISED_SCORE = 60.0


def _attention(q, k, vt, score_bound, tq=128, bounded_blocks=2):
    bsz, n_kv, group, s, hd = q.shape
    n_kt, tk = vt.shape[2], vt.shape[4]
    assert n_kt % ATTN_SLOTS == 0 and n_kt >= 2 * ATTN_SLOTS
    nq = group * tq

    def call(body, scratch, name, tq_step):
        return pl.pallas_call(
            body,
            grid=(bsz, n_kv, s // tq_step),
            in_specs=[
                pl.BlockSpec((None, None, group, tq_step, hd), lambda b, h, i: (b, h, 0, i, 0)),
                pl.BlockSpec((None, None, s, hd), lambda b, h, i: (b, h, 0, 0)),
                pl.BlockSpec((None, None, n_kt, vt.shape[3], tk), lambda b, h, i: (b, h, 0, 0, 0)),
            ],
            out_specs=pl.BlockSpec((None, tq_step, group * hd), lambda b, h, i: (b, i, h)),
            out_shape=jax.ShapeDtypeStruct((bsz, s, n_kv * group * hd), BF16),
            scratch_shapes=scratch,
            compiler_params=_compiler_params(3),
            name=name,
        )(q, k, vt)

    p_slots = [pltpu.VMEM((1, tk, nq), BF16)] * ATTN_SLOTS
    s_slots = [pltpu.VMEM((1, tk, nq), F32)] * ATTN_SLOTS
    return lax.cond(
        score_bound <= MAX_UNSTABILISED_SCORE,
        lambda: call(_attn_bounded_kernel, [pltpu.VMEM((bounded_blocks, hd, nq), F32)] + p_slots,
                     "attention_bounded", bounded_blocks * tq),
        lambda: call(_attn_kernel, [pltpu.VMEM((vt.shape[3], nq), F32)] + s_slots + p_slots,
                     "attention", tq),
    )


def _oproj_kernel(a_ref, w_ref, x_ref, mod_ref, o_ref):
    y = jnp.dot(a_ref[...], w_ref[...], preferred_element_type=F32)
    o_ref[...] = x_ref[...] + mod_ref[2:3, :] * y


def _oproj(a, w_o, x, mod, tm=512):
    bsz, s, d = x.shape
    return pl.pallas_call(
        _oproj_kernel,
        grid=(bsz, s // tm),
        in_specs=[
            pl.BlockSpec((None, tm, a.shape[-1]), lambda b, i: (b, i, 0)),
            pl.BlockSpec(w_o.shape, lambda b, i: (0, 0)),
            pl.BlockSpec((None, tm, d), lambda b, i: (b, i, 0)),
            pl.BlockSpec((None, 6, d), lambda b, i: (b, 0, 0)),
        ],
        out_specs=pl.BlockSpec((None, tm, d), lambda b, i: (b, i, 0)),
        out_shape=jax.ShapeDtypeStruct((bsz, s, d), F32),
        compiler_params=_compiler_params(2),
        name="oproj",
    )(a, w_o, x, mod)


def _ffn_kernel(x_ref, xp_ref, xn_ref, mod_ref, g_ref, wg_ref, wv_ref, cwg_ref, cwv_ref,
                cbg_ref, cbv_ref, wd_ref, gf_ref, o_ref, h_scr, ag_scr, av_scr, *, final_norm):
    tm = x_ref.shape[0]
    halo = xp_ref.shape[0]
    i = pl.program_id(1)
    j = pl.program_id(2)

    @pl.when(j == 0)
    def _():
        gain, scale, shift = g_ref[...], mod_ref[4:5, :], mod_ref[3:4, :]
        h_scr[halo:halo + tm, :] = _modnorm(x_ref[...], gain, scale, shift).astype(BF16)
        hp = _modnorm(xp_ref[...], gain, scale, shift)
        hn = _modnorm(xn_ref[...], gain, scale, shift)
        h_scr[0:halo, :] = jnp.where(i == 0, 0.0, hp).astype(BF16)
        h_scr[halo + tm:, :] = jnp.where(i == pl.num_programs(1) - 1, 0.0, hn).astype(BF16)
        o_ref[...] = jnp.zeros_like(o_ref)

    def conv(a_scr, cw_ref, cb_ref):
        rc = a_scr.shape[0] - 2 * halo
        return (cw_ref[0:1, :] * a_scr[halo - 1:halo - 1 + rc, :]
                + cw_ref[1:2, :] * a_scr[halo:halo + rc, :]
                + cw_ref[2:3, :] * a_scr[halo + 1:halo + 1 + rc, :]
                + cb_ref[...])

    n_chunks = ag_scr.shape[0]
    rc = tm // n_chunks
    for c in range(n_chunks):
        h = h_scr[c * rc:c * rc + rc + 2 * halo, :]
        ag_scr[c] = jnp.dot(h, wg_ref[...], preferred_element_type=F32)
        av_scr[c] = jnp.dot(h, wv_ref[...], preferred_element_type=F32)
    for c in range(n_chunks):
        act = (_gelu(conv(ag_scr.at[c], cwg_ref, cbg_ref))
               * conv(av_scr.at[c], cwv_ref, cbv_ref)).astype(BF16)
        o_ref[c * rc:(c + 1) * rc, :] += jnp.dot(act, wd_ref[...], preferred_element_type=F32)

    @pl.when(j == pl.num_programs(2) - 1)
    def _():
        y = x_ref[...] + mod_ref[5:6, :] * o_ref[...]
        if final_norm:
            y = (y * _rms_scale(y)) * gf_ref[...]
        o_ref[...] = y


def _ffn(x, mod, g, layer, w_up, conv_w, conv_b, w_down, g_final, final_norm,
         tm=512, tf=512, row_chunks=2):
    bsz, s, d = x.shape
    dff = w_down.shape[1]
    nj = dff // tf
    halo = V7X_SUBLANES
    nhb = tm // halo
    last_hb = s // halo - 1
    kernel = functools.partial(_ffn_kernel, final_norm=final_norm)
    return pl.pallas_call(
        kernel,
        grid=(bsz, s // tm, nj),
        in_specs=[
            pl.BlockSpec((None, tm, d), lambda b, i, j: (b, i, 0)),
            pl.BlockSpec((None, halo, d), lambda b, i, j: (b, jnp.maximum(i * nhb - 1, 0), 0)),
            pl.BlockSpec((None, halo, d), lambda b, i, j: (b, jnp.minimum((i + 1) * nhb, last_hb), 0)),
            pl.BlockSpec((None, 6, d), lambda b, i, j: (b, 0, 0)),
            pl.BlockSpec((1, d), lambda b, i, j: (0, 0)),
            pl.BlockSpec((None, d, tf), lambda b, i, j: (layer, 0, j)),
            pl.BlockSpec((None, d, tf), lambda b, i, j: (layer, 0, nj + j)),
            pl.BlockSpec((None, 3, tf), lambda b, i, j: (layer, 0, j)),
            pl.BlockSpec((None, 3, tf), lambda b, i, j: (layer, 0, nj + j)),
            pl.BlockSpec((None, 1, tf), lambda b, i, j: (layer, 0, j)),
            pl.BlockSpec((None, 1, tf), lambda b, i, j: (layer, 0, nj + j)),
            pl.BlockSpec((None, tf, d), lambda b, i, j: (layer, j, 0)),
            pl.BlockSpec((1, d), lambda b, i, j: (0, 0)),
        ],
        out_specs=pl.BlockSpec((None, tm, d), lambda b, i, j: (b, i, 0)),
        out_shape=jax.ShapeDtypeStruct((bsz, s, d), F32),
        scratch_shapes=[
            pltpu.VMEM((tm + 2 * halo, d), BF16),
            pltpu.VMEM((row_chunks, tm // row_chunks + 2 * halo, tf), F32),
            pltpu.VMEM((row_chunks, tm // row_chunks + 2 * halo, tf), F32),
        ],
        compiler_params=_compiler_params(3),
        name="ffn",
    )(x, x, x, mod, g.reshape(1, d), w_up, w_up, conv_w, conv_w,
      conv_b[:, None, :], conv_b[:, None, :], w_down, g_final.reshape(1, d))


def _rope_tables(s, hd):
    axis_dim = hd // 2
    n_rows = s // GRID_W
    inv_freq = ROPE_THETA ** (-jnp.arange(0, axis_dim, 2, dtype=F32) / axis_dim)
    ang_r = jnp.arange(n_rows).astype(F32)[:, None] * inv_freq[None, :]
    ang_c = jnp.arange(GRID_W).astype(F32)[:, None] * inv_freq[None, :]
    zr, zc = jnp.zeros_like(ang_r), jnp.zeros_like(ang_c)
    cos_r = jnp.concatenate([jnp.cos(ang_r)] * 2 + [zr] * 2, axis=-1)
    sin_r = jnp.concatenate([-jnp.sin(ang_r), jnp.sin(ang_r), zr, zr], axis=-1)
    cos_c = jnp.concatenate([zc] * 2 + [jnp.cos(ang_c)] * 2, axis=-1)
    sin_c = jnp.concatenate([zc, zc, -jnp.sin(ang_c), jnp.sin(ang_c)], axis=-1)
    expand = lambda by_row, by_col: (by_row[:, None, :] + by_col[None, :, :]).reshape(s, hd)
    return expand(cos_r, cos_c), expand(sin_r, sin_c)


def kernel(x, c, w_ada, b_ada, g_norm, g_final, a_w_in, a_g_v, a_w_s, a_b_s, a_w_out,
           b_w_qkv, b_g_q, b_g_k, b_w_o, f_w_up, f_conv_w, f_conv_b, f_w_down):
    depth = w_ada.shape[0]
    bsz, s, d = x.shape
    hd = b_g_q.shape[-1]
    n_heads = d // hd
    n_kv = (b_w_qkv.shape[-1] // hd - n_heads) // 2
    n_mixers = 2

    mod = _adaln(c, w_ada, b_ada)
    w_up_bf16 = f_w_up.astype(BF16)
    w_down_bf16 = f_w_down.astype(BF16)
    cos, sin = _rope_tables(s, hd)

    for i in range(depth):
        j = i // n_mixers
        if i % n_mixers == 0:
            n_groups, chunk, _ = a_w_s.shape[1:]
            width = a_w_out.shape[1]
            bs_full = jnp.repeat(jnp.transpose(a_b_s[j]), width // n_groups, axis=1)
            u, vn = _gmlp_in(x, mod[i], g_norm[i, 0], a_w_in[j].astype(BF16), a_g_v[j])
            x = _gmlp_out(u, vn, a_w_s[j].astype(BF16), bs_full, x, mod[i], a_w_out[j].astype(BF16))
        else:
            w = b_w_qkv[j]
            qk_cols = (n_heads + n_kv) * hd
            q, k, vt = _qkv(x, mod[i], g_norm[i, 0], w[:, :qk_cols].astype(BF16),
                            jnp.transpose(w[:, qk_cols:]).astype(BF16), b_g_q[j], b_g_k[j],
                            cos, sin, n_kv)
            score_bound = (1.01 * math.log2(math.e) * math.sqrt(hd)
                           * jnp.max(jnp.abs(b_g_q[j])) * jnp.max(jnp.abs(b_g_k[j])))
            a = _attention(q, k, vt, score_bound)
            x = _oproj(a, b_w_o[j].astype(BF16), x, mod[i])
        x = _ffn(x, mod[i], g_norm[i, 1], i, w_up_bf16, f_conv_w, f_conv_b, w_down_bf16,
                 g_final, final_norm=(i == depth - 1))
    return x
```

```python
import functools
import math

import jax
import jax.numpy as jnp
from jax import lax
from jax.experimental import pallas as pl
from jax.experimental.pallas import tpu as pltpu

EPS = 1e-6
GRID_W = 64
ROPE_THETA = 10000.0
V7X_VMEM_BYTES = 64 * 1024 * 1024
V7X_SUBLANES = 8
V7X_BF16_SUBLANES = 16
VMEM_LIMIT_BYTES = V7X_VMEM_BYTES - 8 * 1024 * 1024

BF16 = jnp.bfloat16
F32 = jnp.float32


def _compiler_params(n_grid_axes):
    return pltpu.CompilerParams(
        dimension_semantics=("arbitrary",) * n_grid_axes,
        vmem_limit_bytes=VMEM_LIMIT_BYTES,
    )


def _rms_scale(x):
    return lax.rsqrt(jnp.mean(x * x, axis=-1, keepdims=True) + EPS)


def _modnorm(x, gain, scale, shift):
    return (x * _rms_scale(x)) * (gain * (1.0 + scale)) + shift


def _gelu(x):
    return 0.5 * x * (1.0 + lax.erf(x * (1.0 / math.sqrt(2.0))))


def _adaln_kernel(c_ref, w_ref, b_ref, o_ref):
    c = c_ref[...]
    cond = c * (1.0 / (1.0 + jnp.exp(-c)))
    o_ref[...] = jnp.dot(cond, w_ref[...], precision=lax.Precision.HIGHEST,
                         preferred_element_type=F32) + b_ref[...]


def _adaln(c, w_ada, b_ada, tn=1024):
    depth, d, n = w_ada.shape
    bsz = c.shape[0]
    rows = -(-bsz // V7X_SUBLANES) * V7X_SUBLANES
    c_pad = jnp.zeros((rows, d), F32).at[:bsz].set(c)
    out = pl.pallas_call(
        _adaln_kernel,
        grid=(depth, n // tn),
        in_specs=[
            pl.BlockSpec((rows, d), lambda l, j: (0, 0)),
            pl.BlockSpec((None, d, tn), lambda l, j: (l, 0, j)),
            pl.BlockSpec((None, 1, tn), lambda l, j: (l, 0, j)),
        ],
        out_specs=pl.BlockSpec((None, rows, tn), lambda l, j: (l, 0, j)),
        out_shape=jax.ShapeDtypeStruct((depth, rows, n), F32),
        compiler_params=_compiler_params(2),
        name="adaln",
    )(c_pad, w_ada, b_ada.reshape(depth, 1, n))
    return out[:, :bsz].reshape(depth, bsz, 6, d)


def _gmlp_in_kernel(x_ref, mod_ref, g_ref, w_ref, gv_ref, u_ref, vn_ref, h_scr, v_scr, *, nc):
    width = u_ref.shape[-1]
    h_scr[...] = _modnorm(x_ref[...], g_ref[...], mod_ref[1:2, :], mod_ref[0:1, :]).astype(BF16)
    ssq = jnp.zeros((x_ref.shape[0], 1), F32)
    for c in range(width // nc):
        cols = slice(c * nc, (c + 1) * nc)
        zu = _gelu(jnp.dot(h_scr[...], w_ref[:, cols], preferred_element_type=F32))
        u_ref[:, cols] = zu.astype(BF16)
        zv = _gelu(jnp.dot(h_scr[...], w_ref[:, width + c * nc: width + (c + 1) * nc],
                           preferred_element_type=F32))
        v_scr[:, cols] = zv
        ssq = ssq + jnp.sum(zv * zv, axis=-1, keepdims=True)
    r = lax.rsqrt(ssq * (1.0 / width) + EPS)
    vn_ref[...] = ((v_scr[...] * r) * gv_ref[...]).astype(BF16)


def _gmlp_in(x, mod, g, w_in, g_v, tm=512, nc=512):
    bsz, s, d = x.shape
    width = w_in.shape[1] // 2
    kernel = functools.partial(_gmlp_in_kernel, nc=nc)
    return pl.pallas_call(
        kernel,
        grid=(bsz, s // tm),
        in_specs=[
            pl.BlockSpec((None, tm, d), lambda b, i: (b, i, 0)),
            pl.BlockSpec((None, 6, d), lambda b, i: (b, 0, 0)),
            pl.BlockSpec((1, d), lambda b, i: (0, 0)),
            pl.BlockSpec((d, 2 * width), lambda b, i: (0, 0)),
            pl.BlockSpec((1, width), lambda b, i: (0, 0)),
        ],
        out_specs=[
            pl.BlockSpec((None, tm, width), lambda b, i: (b, i, 0)),
            pl.BlockSpec((None, tm, width), lambda b, i: (b, i, 0)),
        ],
        out_shape=[jax.ShapeDtypeStruct((bsz, s, width), BF16)] * 2,
        scratch_shapes=[pltpu.VMEM((tm, d), BF16), pltpu.VMEM((tm, width), F32)],
        compiler_params=_compiler_params(2),
        name="gmlp_in",
    )(x, mod, g.reshape(1, d), w_in, g_v.reshape(1, width))


def _gmlp_out_kernel(u_ref, vn_ref, ws_ref, bs_ref, x_ref, mod_ref, wo_ref, o_ref, y_scr):
    n_groups, chunk, _ = ws_ref.shape
    gd = u_ref.shape[-1] // n_groups
    for n in range(u_ref.shape[0] // chunk):
        rows = slice(n * chunk, (n + 1) * chunk)
        for g in range(n_groups):
            cols = slice(g * gd, (g + 1) * gd)
            sv = jnp.dot(ws_ref[g], vn_ref[rows, cols], preferred_element_type=F32) + bs_ref[:, cols]
            y_scr[rows, cols] = (u_ref[rows, cols].astype(F32) * sv).astype(BF16)
    y = jnp.dot(y_scr[...], wo_ref[...], preferred_element_type=F32)
    o_ref[...] = x_ref[...] + mod_ref[2:3, :] * y


def _gmlp_out(u, vn, w_s, bs_full, x, mod, w_out, tm=512):
    bsz, s, d = x.shape
    width = u.shape[-1]
    n_groups, chunk, _ = w_s.shape
    return pl.pallas_call(
        _gmlp_out_kernel,
        grid=(bsz, s // tm),
        in_specs=[
            pl.BlockSpec((None, tm, width), lambda b, i: (b, i, 0)),
            pl.BlockSpec((None, tm, width), lambda b, i: (b, i, 0)),
            pl.BlockSpec((n_groups, chunk, chunk), lambda b, i: (0, 0, 0)),
            pl.BlockSpec((chunk, width), lambda b, i: (0, 0)),
            pl.BlockSpec((None, tm, d), lambda b, i: (b, i, 0)),
            pl.BlockSpec((None, 6, d), lambda b, i: (b, 0, 0)),
            pl.BlockSpec((width, d), lambda b, i: (0, 0)),
        ],
        out_specs=pl.BlockSpec((None, tm, d), lambda b, i: (b, i, 0)),
        out_shape=jax.ShapeDtypeStruct((bsz, s, d), F32),
        scratch_shapes=[pltpu.VMEM((tm, width), BF16)],
        compiler_params=_compiler_params(2),
        name="gmlp_out",
    )(u, vn, w_s, bs_full, x, mod, w_out)


def _rope(t, cos, sin_signed, half_idx):
    quarter = t.shape[-1] // 4
    partner = jnp.where(half_idx, pltpu.roll(t, t.shape[-1] - quarter, 1), pltpu.roll(t, quarter, 1))
    return t * cos + partner * sin_signed


def _qkv_kernel(x_ref, mod_ref, g_ref, wqk_ref, wvt_ref, gq_ref, gk_ref, cos_ref, sin_ref,
                q_ref, k_ref, vt_ref, *, q_scale, row_chunk):
    n_kv, group, tm, hd = q_ref.shape
    lane = lax.broadcasted_iota(jnp.int32, (row_chunk, hd), 1)
    half_idx = (lane % (hd // 2)) < (hd // 4)
    pad_rows = vt_ref.shape[2] - hd
    ones_row = (lax.broadcasted_iota(jnp.int32, (pad_rows, row_chunk), 0) == 0).astype(BF16)

    for r0 in range(0, tm, row_chunk):
        rows = slice(r0, r0 + row_chunk)
        h = _modnorm(x_ref[rows, :], g_ref[...], mod_ref[1:2, :], mod_ref[0:1, :]).astype(BF16)
        cos = cos_ref[rows, :]
        sin = sin_ref[rows, :]

        def head(t, gain, post):
            t = t * lax.rsqrt(jnp.mean(t * t, axis=-1, keepdims=True) + EPS) * gain
            return (_rope(t, cos, sin, half_idx) * post).astype(BF16)

        for kv in range(n_kv):
            cols = slice(kv * group * hd, (kv + 1) * group * hd)
            tq = jnp.dot(h, wqk_ref[:, cols], preferred_element_type=F32)
            for g in range(group):
                q_ref[kv, g, rows, :] = head(tq[:, g * hd:(g + 1) * hd], gq_ref[...], q_scale)
        k_off = n_kv * group * hd
        tk = jnp.dot(h, wqk_ref[:, k_off:k_off + n_kv * hd], preferred_element_type=F32)
        for kv in range(n_kv):
            k_ref[kv, rows, :] = head(tk[:, kv * hd:(kv + 1) * hd], gk_ref[...], 1.0)
        vt = lax.dot_general(wvt_ref[...], h, (((1,), (1,)), ((), ())),
                             preferred_element_type=F32)
        for kv in range(n_kv):
            vt_ref[kv, 0, :hd, rows] = vt[kv * hd:(kv + 1) * hd, :].astype(BF16)
            vt_ref[kv, 0, hd:, rows] = ones_row


def _qkv(x, mod, g, w_qk, w_vt, g_q, g_k, cos, sin, n_kv, tm=512, row_chunk=256):
    bsz, s, d = x.shape
    hd = g_q.shape[-1]
    n_heads = d // hd
    group = n_heads // n_kv
    kernel = functools.partial(_qkv_kernel, q_scale=math.log2(math.e) / math.sqrt(hd),
                               row_chunk=row_chunk)
    return pl.pallas_call(
        kernel,
        grid=(bsz, s // tm),
        in_specs=[
            pl.BlockSpec((None, tm, d), lambda b, i: (b, i, 0)),
            pl.BlockSpec((None, 6, d), lambda b, i: (b, 0, 0)),
            pl.BlockSpec((1, d), lambda b, i: (0, 0)),
            pl.BlockSpec(w_qk.shape, lambda b, i: (0, 0)),
            pl.BlockSpec(w_vt.shape, lambda b, i: (0, 0)),
            pl.BlockSpec((1, hd), lambda b, i: (0, 0)),
            pl.BlockSpec((1, hd), lambda b, i: (0, 0)),
            pl.BlockSpec((tm, hd), lambda b, i: (i, 0)),
            pl.BlockSpec((tm, hd), lambda b, i: (i, 0)),
        ],
        out_specs=[
            pl.BlockSpec((None, n_kv, group, tm, hd), lambda b, i: (b, 0, 0, i, 0)),
            pl.BlockSpec((None, n_kv, tm, hd), lambda b, i: (b, 0, i, 0)),
            pl.BlockSpec((None, n_kv, 1, hd + V7X_BF16_SUBLANES, tm), lambda b, i: (b, 0, i, 0, 0)),
        ],
        out_shape=[
            jax.ShapeDtypeStruct((bsz, n_kv, group, s, hd), BF16),
            jax.ShapeDtypeStruct((bsz, n_kv, s, hd), BF16),
            jax.ShapeDtypeStruct((bsz, n_kv, s // tm, hd + V7X_BF16_SUBLANES, tm), BF16),
        ],
        compiler_params=_compiler_params(2),
        name="qkv",
    )(x, mod, g.reshape(1, d), w_qk, w_vt, g_q.reshape(1, hd), g_k.reshape(1, hd), cos, sin)


ATTN_SLOTS = 4


def _attn_kernel(q_ref, k_ref, vt_ref, o_ref, acc_scr, *slots):
    group, tq, hd = q_ref.shape
    n_kt, _, tk = vt_ref.shape
    nq = group * tq
    s_scr, p_scr = slots[:ATTN_SLOTS], slots[ATTN_SLOTS:]
    st_idx = jnp.minimum(pl.program_id(2), 0)
    ld_idx = jnp.minimum(pl.program_id(1), 0)

    def scores(kt, s_scr):
        k_tile = k_ref[pl.ds(pl.multiple_of(kt * tk, tk), tk), :]
        s = lax.dot_general(k_tile, q_ref[...].reshape(nq, hd), (((1,), (1,)), ((), ())),
                            preferred_element_type=F32)
        s_scr[st_idx] = s
        return jnp.max(s, axis=0, keepdims=True)

    def softmax(s_scr, p_scr, m, tile_max):
        m_new = jnp.maximum(m, tile_max)
        p_scr[st_idx] = jnp.exp2(s_scr[ld_idx] - m_new).astype(BF16)
        return m_new, jnp.exp2(m - m_new)

    def values(kt, p_scr, alpha):
        pv = jnp.dot(vt_ref[kt], p_scr[ld_idx], preferred_element_type=F32)
        acc_scr[...] = alpha * acc_scr[...] + pv

    def tile(t, slot, carry, prefetch):
        m, max_t, max_t1, alpha_t2, alpha_t1 = carry
        max_t2 = scores(t + 2, s_scr[(slot + 2) % ATTN_SLOTS]) if prefetch else None
        if alpha_t2 is not None:
            values(t - 2, p_scr[(slot - 2) % ATTN_SLOTS], alpha_t2)
        m, alpha = softmax(s_scr[slot], p_scr[slot], m, max_t)
        return m, max_t1, max_t2, alpha_t1, alpha

    def group_of_tiles(t0, carry, n_prefetch=ATTN_SLOTS):
        for slot in range(ATTN_SLOTS):
            carry = tile(t0 + slot, slot, carry, prefetch=slot < n_prefetch)
        return carry

    n_groups = n_kt // ATTN_SLOTS
    acc_scr[...] = jnp.zeros_like(acc_scr)
    m0 = jnp.full((1, nq), -jnp.inf, F32)
    carry = (m0, scores(0, s_scr[0]), scores(1, s_scr[1]), None, None)
    carry = group_of_tiles(0, carry)
    carry = lax.fori_loop(1, n_groups - 1, lambda g, c: group_of_tiles(g * ATTN_SLOTS, c), carry,
                          unroll=True)
    _, _, _, alpha_t2, alpha_t1 = group_of_tiles(n_kt - ATTN_SLOTS, carry,
                                                 n_prefetch=ATTN_SLOTS - 2)
    values(n_kt - 2, p_scr[ATTN_SLOTS - 2], alpha_t2)
    values(n_kt - 1, p_scr[ATTN_SLOTS - 1], alpha_t1)
    out_t = acc_scr[:hd, :] / acc_scr[hd:hd + 1, :]
    for g in range(group):
        o_ref[:, g * hd:(g + 1) * hd] = out_t[:, g * tq:(g + 1) * tq].T.astype(BF16)


def _attn_bounded_kernel(q_ref, k_ref, vt_ref, o_ref, acc_scr, *p_scr):
    group, tq_step, hd = q_ref.shape
    n_kt, _, tk = vt_ref.shape
    n_blocks = acc_scr.shape[0]
    tq = tq_step // n_blocks
    nq = group * tq
    st_idx = jnp.minimum(pl.program_id(2), 0)
    ld_idx = jnp.minimum(pl.program_id(1), 0)

    for blk in range(n_blocks):
        q_rows = slice(blk * tq, (blk + 1) * tq)
        acc = acc_scr.at[blk]

        def probabilities(kt):
            k_tile = k_ref[kt * tk:(kt + 1) * tk, :]
            s = lax.dot_general(k_tile, q_ref[:, q_rows, :].reshape(nq, hd),
                                (((1,), (1,)), ((), ())), preferred_element_type=F32)
            p = jnp.exp2(s)
            p_scr[kt % ATTN_SLOTS][st_idx] = p.astype(BF16)
            return jnp.sum(p, axis=0, keepdims=True)

        def values(kt):
            acc[...] += jnp.dot(vt_ref[kt, :hd, :], p_scr[kt % ATTN_SLOTS][ld_idx],
                                preferred_element_type=F32)

        acc[...] = jnp.zeros_like(acc)
        l = probabilities(0) + probabilities(1)
        for kt in range(n_kt):
            if kt + 2 < n_kt:
                l = l + probabilities(kt + 2)
            values(kt)
        out_t = acc[...] / l
        for g in range(group):
            o_ref[q_rows, g * hd:(g + 1) * hd] = out_t[:, g * tq:(g + 1) * tq].T.astype(BF16)


MAX_UNSTABILISED_SCORE = 60.0


def _attention(q, k, vt, score_bound, tq=128, bounded_blocks=2):
    bsz, n_kv, group, s, hd = q.shape
    n_kt, tk = vt.shape[2], vt.shape[4]
    assert n_kt % ATTN_SLOTS == 0 and n_kt >= 2 * ATTN_SLOTS
    nq = group * tq

    def call(body, scratch, name, tq_step):
        return pl.pallas_call(
            body,
            grid=(bsz, n_kv, s // tq_step),
            in_specs=[
                pl.BlockSpec((None, None, group, tq_step, hd), lambda b, h, i: (b, h, 0, i, 0)),
                pl.BlockSpec((None, None, s, hd), lambda b, h, i: (b, h, 0, 0)),
                pl.BlockSpec((None, None, n_kt, vt.shape[3], tk), lambda b, h, i: (b, h, 0, 0, 0)),
            ],
            out_specs=pl.BlockSpec((None, tq_step, group * hd), lambda b, h, i: (b, i, h)),
            out_shape=jax.ShapeDtypeStruct((bsz, s, n_kv * group * hd), BF16),
            scratch_shapes=scratch,
            compiler_params=_compiler_params(3),
            name=name,
        )(q, k, vt)

    p_slots = [pltpu.VMEM((1, tk, nq), BF16)] * ATTN_SLOTS
    s_slots = [pltpu.VMEM((1, tk, nq), F32)] * ATTN_SLOTS
    return lax.cond(
        score_bound <= MAX_UNSTABILISED_SCORE,
        lambda: call(_attn_bounded_kernel, [pltpu.VMEM((bounded_blocks, hd, nq), F32)] + p_slots,
                     "attention_bounded", bounded_blocks * tq),
        lambda: call(_attn_kernel, [pltpu.VMEM((vt.shape[3], nq), F32)] + s_slots + p_slots,
                     "attention", tq),
    )


def _oproj_kernel(a_ref, w_ref, x_ref, mod_ref, o_ref):
    y = jnp.dot(a_ref[...], w_ref[...], preferred_element_type=F32)
    o_ref[...] = x_ref[...] + mod_ref[2:3, :] * y


def _oproj(a, w_o, x, mod, tm=512):
    bsz, s, d = x.shape
    return pl.pallas_call(
        _oproj_kernel,
        grid=(bsz, s // tm),
        in_specs=[
            pl.BlockSpec((None, tm, a.shape[-1]), lambda b, i: (b, i, 0)),
            pl.BlockSpec(w_o.shape, lambda b, i: (0, 0)),
            pl.BlockSpec((None, tm, d), lambda b, i: (b, i, 0)),
            pl.BlockSpec((None, 6, d), lambda b, i: (b, 0, 0)),
        ],
        out_specs=pl.BlockSpec((None, tm, d), lambda b, i: (b, i, 0)),
        out_shape=jax.ShapeDtypeStruct((bsz, s, d), F32),
        compiler_params=_compiler_params(2),
        name="oproj",
    )(a, w_o, x, mod)


def _ffn_kernel(x_ref, xp_ref, xn_ref, mod_ref, g_ref, wg_ref, wv_ref, cwg_ref, cwv_ref,
                cbg_ref, cbv_ref, wd_ref, gf_ref, o_ref, h_scr, ag_scr, av_scr, *, final_norm, n_chunks):
    tm = x_ref.shape[0]
    halo = xp_ref.shape[0]
    i = pl.program_id(1)
    j = pl.program_id(2)

    @pl.when(j == 0)
    def _():
        gain, scale, shift = g_ref[...], mod_ref[4:5, :], mod_ref[3:4, :]
        h_scr[halo:halo + tm, :] = _modnorm(x_ref[...], gain, scale, shift).astype(BF16)
        hp = _modnorm(xp_ref[...], gain, scale, shift)
        hn = _modnorm(xn_ref[...], gain, scale, shift)
        h_scr[0:halo, :] = jnp.where(i == 0, 0.0, hp).astype(BF16)
        h_scr[halo + tm:, :] = jnp.where(i == pl.num_programs(1) - 1, 0.0, hn).astype(BF16)
        o_ref[...] = jnp.zeros_like(o_ref)

    rc = tm // n_chunks

    def conv(a_scr, cw_ref, cb_ref, row0):
        base = halo + row0
        return (cw_ref[0:1, :] * a_scr[base - 1:base - 1 + rc, :]
                + cw_ref[1:2, :] * a_scr[base:base + rc, :]
                + cw_ref[2:3, :] * a_scr[base + 1:base + 1 + rc, :]
                + cb_ref[...])

    edges = [0] + [c * rc + 2 * halo for c in range(1, n_chunks)] + [tm + 2 * halo]
    for c in range(n_chunks):
        rows = slice(edges[c], edges[c + 1])
        ag_scr[rows, :] = jnp.dot(h_scr[rows, :], wg_ref[...], preferred_element_type=F32)
        av_scr[rows, :] = jnp.dot(h_scr[rows, :], wv_ref[...], preferred_element_type=F32)
    for c in range(n_chunks):
        act = (_gelu(conv(ag_scr, cwg_ref, cbg_ref, c * rc))
               * conv(av_scr, cwv_ref, cbv_ref, c * rc)).astype(BF16)
        o_ref[c * rc:(c + 1) * rc, :] += jnp.dot(act, wd_ref[...], preferred_element_type=F32)

    @pl.when(j == pl.num_programs(2) - 1)
    def _():
        y = x_ref[...] + mod_ref[5:6, :] * o_ref[...]
        if final_norm:
            y = (y * _rms_scale(y)) * gf_ref[...]
        o_ref[...] = y


def _ffn(x, mod, g, layer, w_up, conv_w, conv_b, w_down, g_final, final_norm,
         tm=512, tf=512, row_chunks=2):
    bsz, s, d = x.shape
    dff = w_down.shape[1]
    nj = dff // tf
    halo = V7X_SUBLANES
    nhb = tm // halo
    last_hb = s // halo - 1
    kernel = functools.partial(_ffn_kernel, final_norm=final_norm, n_chunks=row_chunks)
    return pl.pallas_call(
        kernel,
        grid=(bsz, s // tm, nj),
        in_specs=[
            pl.BlockSpec((None, tm, d), lambda b, i, j: (b, i, 0)),
            pl.BlockSpec((None, halo, d), lambda b, i, j: (b, jnp.maximum(i * nhb - 1, 0), 0)),
            pl.BlockSpec((None, halo, d), lambda b, i, j: (b, jnp.minimum((i + 1) * nhb, last_hb), 0)),
            pl.BlockSpec((None, 6, d), lambda b, i, j: (b, 0, 0)),
            pl.BlockSpec((1, d), lambda b, i, j: (0, 0)),
            pl.BlockSpec((None, d, tf), lambda b, i, j: (layer, 0, j)),
            pl.BlockSpec((None, d, tf), lambda b, i, j: (layer, 0, nj + j)),
            pl.BlockSpec((None, 3, tf), lambda b, i, j: (layer, 0, j)),
            pl.BlockSpec((None, 3, tf), lambda b, i, j: (layer, 0, nj + j)),
            pl.BlockSpec((None, 1, tf), lambda b, i, j: (layer, 0, j)),
            pl.BlockSpec((None, 1, tf), lambda b, i, j: (layer, 0, nj + j)),
            pl.BlockSpec((None, tf, d), lambda b, i, j: (layer, j, 0)),
            pl.BlockSpec((1, d), lambda b, i, j: (0, 0)),
        ],
        out_specs=pl.BlockSpec((None, tm, d), lambda b, i, j: (b, i, 0)),
        out_shape=jax.ShapeDtypeStruct((bsz, s, d), F32),
        scratch_shapes=[
            pltpu.VMEM((tm + 2 * halo, d), BF16),
            pltpu.VMEM((tm + 2 * halo, tf), F32),
            pltpu.VMEM((tm + 2 * halo, tf), F32),
        ],
        compiler_params=_compiler_params(3),
        name="ffn",
    )(x, x, x, mod, g.reshape(1, d), w_up, w_up, conv_w, conv_w,
      conv_b[:, None, :], conv_b[:, None, :], w_down, g_final.reshape(1, d))


def _rope_tables(s, hd):
    axis_dim = hd // 2
    n_rows = s // GRID_W
    inv_freq = ROPE_THETA ** (-jnp.arange(0, axis_dim, 2, dtype=F32) / axis_dim)
    ang_r = jnp.arange(n_rows).astype(F32)[:, None] * inv_freq[None, :]
    ang_c = jnp.arange(GRID_W).astype(F32)[:, None] * inv_freq[None, :]
    zr, zc = jnp.zeros_like(ang_r), jnp.zeros_like(ang_c)
    cos_r = jnp.concatenate([jnp.cos(ang_r)] * 2 + [zr] * 2, axis=-1)
    sin_r = jnp.concatenate([-jnp.sin(ang_r), jnp.sin(ang_r), zr, zr], axis=-1)
    cos_c = jnp.concatenate([zc] * 2 + [jnp.cos(ang_c)] * 2, axis=-1)
    sin_c = jnp.concatenate([zc, zc, -jnp.sin(ang_c), jnp.sin(ang_c)], axis=-1)
    expand = lambda by_row, by_col: (by_row[:, None, :] + by_col[None, :, :]).reshape(s, hd)
    return expand(cos_r, cos_c), expand(sin_r, sin_c)


def kernel(x, c, w_ada, b_ada, g_norm, g_final, a_w_in, a_g_v, a_w_s, a_b_s, a_w_out,
           b_w_qkv, b_g_q, b_g_k, b_w_o, f_w_up, f_conv_w, f_conv_b, f_w_down):
    depth = w_ada.shape[0]
    bsz, s, d = x.shape
    hd = b_g_q.shape[-1]
    n_heads = d // hd
    n_kv = (b_w_qkv.shape[-1] // hd - n_heads) // 2
    n_mixers = 2

    mod = _adaln(c, w_ada, b_ada)
    w_up_bf16 = f_w_up.astype(BF16)
    w_down_bf16 = f_w_down.astype(BF16)
    cos, sin = _rope_tables(s, hd)

    for i in range(depth):
        j = i // n_mixers
        if i % n_mixers == 0:
            n_groups, chunk, _ = a_w_s.shape[1:]
            width = a_w_out.shape[1]
            bs_full = jnp.repeat(jnp.transpose(a_b_s[j]), width // n_groups, axis=1)
            u, vn = _gmlp_in(x, mod[i], g_norm[i, 0], a_w_in[j].astype(BF16), a_g_v[j])
            x = _gmlp_out(u, vn, a_w_s[j].astype(BF16), bs_full, x, mod[i], a_w_out[j].astype(BF16))
        else:
            w = b_w_qkv[j]
            qk_cols = (n_heads + n_kv) * hd
            q, k, vt = _qkv(x, mod[i], g_norm[i, 0], w.astype(BF16),
                            jnp.transpose(w[:, qk_cols:]).astype(BF16), b_g_q[j], b_g_k[j],
                            cos, sin, n_kv)
            score_bound = (1.01 * math.log2(math.e) * math.sqrt(hd)
                           * jnp.max(jnp.abs(b_g_q[j])) * jnp.max(jnp.abs(b_g_k[j])))
            a = _attention(q, k, vt, score_bound)
            x = _oproj(a, b_w_o[j].astype(BF16), x, mod[i])
        x = _ffn(x, mod[i], g_norm[i, 1], i, w_up_bf16, f_conv_w, f_conv_b, w_down_bf16,
                 g_final, final_norm=(i == depth - 1))
    return x
```

```python
import functools
import math

import jax
import jax.numpy as jnp
from jax import lax
from jax.experimental import pallas as pl
from jax.experimental.pallas import tpu as pltpu

EPS = 1e-6
GRID_W = 64
ROPE_THETA = 10000.0
V7X_VMEM_BYTES = 64 * 1024 * 1024
V7X_SUBLANES = 8
V7X_BF16_SUBLANES = 16
VMEM_LIMIT_BYTES = V7X_VMEM_BYTES - 8 * 1024 * 1024

BF16 = jnp.bfloat16
F32 = jnp.float32


def _compiler_params(n_grid_axes):
    return pltpu.CompilerParams(
        dimension_semantics=("arbitrary",) * n_grid_axes,
        vmem_limit_bytes=VMEM_LIMIT_BYTES,
    )


def _rms_scale(x):
    return lax.rsqrt(jnp.mean(x * x, axis=-1, keepdims=True) + EPS)


def _modnorm(x, gain, scale, shift):
    return (x * _rms_scale(x)) * (gain * (1.0 + scale)) + shift


def _gelu(x):
    return 0.5 * x * (1.0 + lax.erf(x * (1.0 / math.sqrt(2.0))))


def _adaln_kernel(c_ref, w_ref, b_ref, o_ref):
    c = c_ref[...]
    cond = c * (1.0 / (1.0 + jnp.exp(-c)))
    o_ref[...] = jnp.dot(cond, w_ref[...], precision=lax.Precision.HIGHEST,
                         preferred_element_type=F32) + b_ref[...]


def _adaln(c, w_ada, b_ada, tn=1024):
    depth, d, n = w_ada.shape
    bsz = c.shape[0]
    rows = -(-bsz // V7X_SUBLANES) * V7X_SUBLANES
    c_pad = jnp.zeros((rows, d), F32).at[:bsz].set(c)
    out = pl.pallas_call(
        _adaln_kernel,
        grid=(depth, n // tn),
        in_specs=[
            pl.BlockSpec((rows, d), lambda l, j: (0, 0)),
            pl.BlockSpec((None, d, tn), lambda l, j: (l, 0, j)),
            pl.BlockSpec((None, 1, tn), lambda l, j: (l, 0, j)),
        ],
        out_specs=pl.BlockSpec((None, rows, tn), lambda l, j: (l, 0, j)),
        out_shape=jax.ShapeDtypeStruct((depth, rows, n), F32),
        compiler_params=_compiler_params(2),
        name="adaln",
    )(c_pad, w_ada, b_ada.reshape(depth, 1, n))
    return out[:, :bsz].reshape(depth, bsz, 6, d)


def _gmlp_in_kernel(x_ref, mod_ref, g_ref, w_ref, gv_ref, u_ref, vn_ref, h_scr, v_scr, *, nc):
    width = u_ref.shape[-1]
    h_scr[...] = _modnorm(x_ref[...], g_ref[...], mod_ref[1:2, :], mod_ref[0:1, :]).astype(BF16)
    ssq = jnp.zeros((x_ref.shape[0], 1), F32)
    for c in range(width // nc):
        cols = slice(c * nc, (c + 1) * nc)
        zu = _gelu(jnp.dot(h_scr[...], w_ref[:, cols], preferred_element_type=F32))
        u_ref[:, cols] = zu.astype(BF16)
        zv = _gelu(jnp.dot(h_scr[...], w_ref[:, width + c * nc: width + (c + 1) * nc],
                           preferred_element_type=F32))
        v_scr[:, cols] = zv
        ssq = ssq + jnp.sum(zv * zv, axis=-1, keepdims=True)
    r = lax.rsqrt(ssq * (1.0 / width) + EPS)
    vn_ref[...] = ((v_scr[...] * r) * gv_ref[...]).astype(BF16)


def _gmlp_in(x, mod, g, w_in, g_v, tm=512, nc=512):
    bsz, s, d = x.shape
    width = w_in.shape[1] // 2
    kernel = functools.partial(_gmlp_in_kernel, nc=nc)
    return pl.pallas_call(
        kernel,
        grid=(bsz, s // tm),
        in_specs=[
            pl.BlockSpec((None, tm, d), lambda b, i: (b, i, 0)),
            pl.BlockSpec((None, 6, d), lambda b, i: (b, 0, 0)),
            pl.BlockSpec((1, d), lambda b, i: (0, 0)),
            pl.BlockSpec((d, 2 * width), lambda b, i: (0, 0)),
            pl.BlockSpec((1, width), lambda b, i: (0, 0)),
        ],
        out_specs=[
            pl.BlockSpec((None, tm, width), lambda b, i: (b, i, 0)),
            pl.BlockSpec((None, tm, width), lambda b, i: (b, i, 0)),
        ],
        out_shape=[jax.ShapeDtypeStruct((bsz, s, width), BF16)] * 2,
        scratch_shapes=[pltpu.VMEM((tm, d), BF16), pltpu.VMEM((tm, width), F32)],
        compiler_params=_compiler_params(2),
        name="gmlp_in",
    )(x, mod, g.reshape(1, d), w_in, g_v.reshape(1, width))


def _gmlp_out_kernel(u_ref, vn_ref, ws_ref, bs_ref, x_ref, mod_ref, wo_ref, o_ref, y_scr):
    n_groups, chunk, _ = ws_ref.shape
    gd = u_ref.shape[-1] // n_groups
    for n in range(u_ref.shape[0] // chunk):
        rows = slice(n * chunk, (n + 1) * chunk)
        for g in range(n_groups):
            cols = slice(g * gd, (g + 1) * gd)
            sv = jnp.dot(ws_ref[g], vn_ref[rows, cols], preferred_element_type=F32) + bs_ref[:, cols]
            y_scr[rows, cols] = (u_ref[rows, cols].astype(F32) * sv).astype(BF16)
    y = jnp.dot(y_scr[...], wo_ref[...], preferred_element_type=F32)
    o_ref[...] = x_ref[...] + mod_ref[2:3, :] * y


def _gmlp_out(u, vn, w_s, bs_full, x, mod, w_out, tm=512):
    bsz, s, d = x.shape
    width = u.shape[-1]
    n_groups, chunk, _ = w_s.shape
    return pl.pallas_call(
        _gmlp_out_kernel,
        grid=(bsz, s // tm),
        in_specs=[
            pl.BlockSpec((None, tm, width), lambda b, i: (b, i, 0)),
            pl.BlockSpec((None, tm, width), lambda b, i: (b, i, 0)),
            pl.BlockSpec((n_groups, chunk, chunk), lambda b, i: (0, 0, 0)),
            pl.BlockSpec((chunk, width), lambda b, i: (0, 0)),
            pl.BlockSpec((None, tm, d), lambda b, i: (b, i, 0)),
            pl.BlockSpec((None, 6, d), lambda b, i: (b, 0, 0)),
            pl.BlockSpec((width, d), lambda b, i: (0, 0)),
        ],
        out_specs=pl.BlockSpec((None, tm, d), lambda b, i: (b, i, 0)),
        out_shape=jax.ShapeDtypeStruct((bsz, s, d), F32),
        scratch_shapes=[pltpu.VMEM((tm, width), BF16)],
        compiler_params=_compiler_params(2),
        name="gmlp_out",
    )(u, vn, w_s, bs_full, x, mod, w_out)


def _rope(t, cos, sin_signed, half_idx):
    quarter = t.shape[-1] // 4
    partner = jnp.where(half_idx, pltpu.roll(t, t.shape[-1] - quarter, 1), pltpu.roll(t, quarter, 1))
    return t * cos + partner * sin_signed


def _qkv_kernel(x_ref, mod_ref, g_ref, wqk_ref, wvt_ref, gq_ref, gk_ref, cos_ref, sin_ref,
                q_ref, k_ref, vt_ref, *, q_scale, row_chunk):
    n_kv, group, tm, hd = q_ref.shape
    lane = lax.broadcasted_iota(jnp.int32, (row_chunk, hd), 1)
    half_idx = (lane % (hd // 2)) < (hd // 4)
    pad_rows = vt_ref.shape[2] - hd
    ones_row = (lax.broadcasted_iota(jnp.int32, (pad_rows, row_chunk), 0) == 0).astype(BF16)

    for r0 in range(0, tm, row_chunk):
        rows = slice(r0, r0 + row_chunk)
        h = _modnorm(x_ref[rows, :], g_ref[...], mod_ref[1:2, :], mod_ref[0:1, :]).astype(BF16)
        cos = cos_ref[rows, :]
        sin = sin_ref[rows, :]

        def head(t, gain, post):
            t = t * lax.rsqrt(jnp.mean(t * t, axis=-1, keepdims=True) + EPS) * gain
            return (_rope(t, cos, sin, half_idx) * post).astype(BF16)

        for kv in range(n_kv):
            cols = slice(kv * group * hd, (kv + 1) * group * hd)
            tq = jnp.dot(h, wqk_ref[:, cols], preferred_element_type=F32)
            for g in range(group):
                q_ref[kv, g, rows, :] = head(tq[:, g * hd:(g + 1) * hd], gq_ref[...], q_scale)
        k_off = n_kv * group * hd
        tk = jnp.dot(h, wqk_ref[:, k_off:k_off + n_kv * hd], preferred_element_type=F32)
        for kv in range(n_kv):
            k_ref[kv, rows, :] = head(tk[:, kv * hd:(kv + 1) * hd], gk_ref[...], 1.0)
        vt = lax.dot_general(wvt_ref[...], h, (((1,), (1,)), ((), ())),
                             preferred_element_type=F32)
        for kv in range(n_kv):
            vt_ref[kv, 0, :hd, rows] = vt[kv * hd:(kv + 1) * hd, :].astype(BF16)
            vt_ref[kv, 0, hd:, rows] = ones_row


def _qkv(x, mod, g, w_qk, w_vt, g_q, g_k, cos, sin, n_kv, tm=512, row_chunk=256):
    bsz, s, d = x.shape
    hd = g_q.shape[-1]
    n_heads = d // hd
    group = n_heads // n_kv
    kernel = functools.partial(_qkv_kernel, q_scale=math.log2(math.e) / math.sqrt(hd),
                               row_chunk=row_chunk)
    return pl.pallas_call(
        kernel,
        grid=(bsz, s // tm),
        in_specs=[
            pl.BlockSpec((None, tm, d), lambda b, i: (b, i, 0)),
            pl.BlockSpec((None, 6, d), lambda b, i: (b, 0, 0)),
            pl.BlockSpec((1, d), lambda b, i: (0, 0)),
            pl.BlockSpec(w_qk.shape, lambda b, i: (0, 0)),
            pl.BlockSpec(w_vt.shape, lambda b, i: (0, 0)),
            pl.BlockSpec((1, hd), lambda b, i: (0, 0)),
            pl.BlockSpec((1, hd), lambda b, i: (0, 0)),
            pl.BlockSpec((tm, hd), lambda b, i: (i, 0)),
            pl.BlockSpec((tm, hd), lambda b, i: (i, 0)),
        ],
        out_specs=[
            pl.BlockSpec((None, n_kv, group, tm, hd), lambda b, i: (b, 0, 0, i, 0)),
            pl.BlockSpec((None, n_kv, tm, hd), lambda b, i: (b, 0, i, 0)),
            pl.BlockSpec((None, n_kv, 1, hd + V7X_BF16_SUBLANES, tm), lambda b, i: (b, 0, i, 0, 0)),
        ],
        out_shape=[
            jax.ShapeDtypeStruct((bsz, n_kv, group, s, hd), BF16),
            jax.ShapeDtypeStruct((bsz, n_kv, s, hd), BF16),
            jax.ShapeDtypeStruct((bsz, n_kv, s // tm, hd + V7X_BF16_SUBLANES, tm), BF16),
        ],
        compiler_params=_compiler_params(2),
        name="qkv",
    )(x, mod, g.reshape(1, d), w_qk, w_vt, g_q.reshape(1, hd), g_k.reshape(1, hd), cos, sin)


ATTN_SLOTS = 4


def _attn_kernel(q_ref, k_ref, vt_ref, o_ref, acc_scr, *slots):
    group, tq, hd = q_ref.shape
    n_kt, _, tk = vt_ref.shape
    nq = group * tq
    s_scr, p_scr = slots[:ATTN_SLOTS], slots[ATTN_SLOTS:]
    st_idx = jnp.minimum(pl.program_id(2), 0)
    ld_idx = jnp.minimum(pl.program_id(1), 0)

    def scores(kt, s_scr):
        k_tile = k_ref[pl.ds(pl.multiple_of(kt * tk, tk), tk), :]
        s = lax.dot_general(k_tile, q_ref[...].reshape(nq, hd), (((1,), (1,)), ((), ())),
                            preferred_element_type=F32)
        s_scr[st_idx] = s
        return jnp.max(s, axis=0, keepdims=True)

    def softmax(s_scr, p_scr, m, tile_max):
        m_new = jnp.maximum(m, tile_max)
        p_scr[st_idx] = jnp.exp2(s_scr[ld_idx] - m_new).astype(BF16)
        return m_new, jnp.exp2(m - m_new)

    def values(kt, p_scr, alpha):
        pv = jnp.dot(vt_ref[kt], p_scr[ld_idx], preferred_element_type=F32)
        acc_scr[...] = alpha * acc_scr[...] + pv

    def tile(t, slot, carry, prefetch):
        m, max_t, max_t1, alpha_t2, alpha_t1 = carry
        max_t2 = scores(t + 2, s_scr[(slot + 2) % ATTN_SLOTS]) if prefetch else None
        if alpha_t2 is not None:
            values(t - 2, p_scr[(slot - 2) % ATTN_SLOTS], alpha_t2)
        m, alpha = softmax(s_scr[slot], p_scr[slot], m, max_t)
        return m, max_t1, max_t2, alpha_t1, alpha

    def group_of_tiles(t0, carry, n_prefetch=ATTN_SLOTS):
        for slot in range(ATTN_SLOTS):
            carry = tile(t0 + slot, slot, carry, prefetch=slot < n_prefetch)
        return carry

    n_groups = n_kt // ATTN_SLOTS
    acc_scr[...] = jnp.zeros_like(acc_scr)
    m0 = jnp.full((1, nq), -jnp.inf, F32)
    carry = (m0, scores(0, s_scr[0]), scores(1, s_scr[1]), None, None)
    carry = group_of_tiles(0, carry)
    carry = lax.fori_loop(1, n_groups - 1, lambda g, c: group_of_tiles(g * ATTN_SLOTS, c), carry,
                          unroll=True)
    _, _, _, alpha_t2, alpha_t1 = group_of_tiles(n_kt - ATTN_SLOTS, carry,
                                                 n_prefetch=ATTN_SLOTS - 2)
    values(n_kt - 2, p_scr[ATTN_SLOTS - 2], alpha_t2)
    values(n_kt - 1, p_scr[ATTN_SLOTS - 1], alpha_t1)
    out_t = acc_scr[:hd, :] / acc_scr[hd:hd + 1, :]
    for g in range(group):
        o_ref[:, g * hd:(g + 1) * hd] = out_t[:, g * tq:(g + 1) * tq].T.astype(BF16)


def _attn_bounded_kernel(q_ref, k_ref, vt_ref, o_ref, acc_scr, *p_scr):
    group, tq_step, hd = q_ref.shape
    n_kt, _, tk = vt_ref.shape
    n_blocks = acc_scr.shape[0]
    tq = tq_step // n_blocks
    nq = group * tq
    st_idx = jnp.minimum(pl.program_id(2), 0)
    ld_idx = jnp.minimum(pl.program_id(1), 0)

    for blk in range(n_blocks):
        q_rows = slice(blk * tq, (blk + 1) * tq)
        acc = acc_scr.at[blk]

        def probabilities(kt):
            k_tile = k_ref[kt * tk:(kt + 1) * tk, :]
            s = lax.dot_general(k_tile, q_ref[:, q_rows, :].reshape(nq, hd),
                                (((1,), (1,)), ((), ())), preferred_element_type=F32)
            p = jnp.exp2(s)
            p_scr[kt % ATTN_SLOTS][st_idx] = p.astype(BF16)
            return jnp.sum(p, axis=0, keepdims=True)

        def values(kt):
            acc[...] += jnp.dot(vt_ref[kt, :hd, :], p_scr[kt % ATTN_SLOTS][ld_idx],
                                preferred_element_type=F32)

        acc[...] = jnp.zeros_like(acc)
        l = probabilities(0) + probabilities(1)
        for kt in range(n_kt):
            if kt + 2 < n_kt:
                l = l + probabilities(kt + 2)
            values(kt)
        out_t = acc[...] / l
        for g in range(group):
            o_ref[q_rows, g * hd:(g + 1) * hd] = out_t[:, g * tq:(g + 1) * tq].T.astype(BF16)


MAX_UNSTABILISED_SCORE = 60.0


def _attention(q, k, vt, score_bound, tq=128, bounded_blocks=4):
    bsz, n_kv, group, s, hd = q.shape
    n_kt, tk = vt.shape[2], vt.shape[4]
    assert n_kt % ATTN_SLOTS == 0 and n_kt >= 2 * ATTN_SLOTS
    nq = group * tq

    def call(body, scratch, name, tq_step):
        return pl.pallas_call(
            body,
            grid=(bsz, n_kv, s // tq_step),
            in_specs=[
                pl.BlockSpec((None, None, group, tq_step, hd), lambda b, h, i: (b, h, 0, i, 0)),
                pl.BlockSpec((None, None, s, hd), lambda b, h, i: (b, h, 0, 0)),
                pl.BlockSpec((None, None, n_kt, vt.shape[3], tk), lambda b, h, i: (b, h, 0, 0, 0)),
            ],
            out_specs=pl.BlockSpec((None, tq_step, group * hd), lambda b, h, i: (b, i, h)),
            out_shape=jax.ShapeDtypeStruct((bsz, s, n_kv * group * hd), BF16),
            scratch_shapes=scratch,
            compiler_params=_compiler_params(3),
            name=name,
        )(q, k, vt)

    p_slots = [pltpu.VMEM((1, tk, nq), BF16)] * ATTN_SLOTS
    s_slots = [pltpu.VMEM((1, tk, nq), F32)] * ATTN_SLOTS
    return lax.cond(
        score_bound <= MAX_UNSTABILISED_SCORE,
        lambda: call(_attn_bounded_kernel, [pltpu.VMEM((bounded_blocks, hd, nq), F32)] + p_slots,
                     "attention_bounded", bounded_blocks * tq),
        lambda: call(_attn_kernel, [pltpu.VMEM((vt.shape[3], nq), F32)] + s_slots + p_slots,
                     "attention", tq),
    )


def _oproj_kernel(a_ref, w_ref, x_ref, mod_ref, o_ref):
    y = jnp.dot(a_ref[...], w_ref[...], preferred_element_type=F32)
    o_ref[...] = x_ref[...] + mod_ref[2:3, :] * y


def _oproj(a, w_o, x, mod, tm=512):
    bsz, s, d = x.shape
    return pl.pallas_call(
        _oproj_kernel,
        grid=(bsz, s // tm),
        in_specs=[
            pl.BlockSpec((None, tm, a.shape[-1]), lambda b, i: (b, i, 0)),
            pl.BlockSpec(w_o.shape, lambda b, i: (0, 0)),
            pl.BlockSpec((None, tm, d), lambda b, i: (b, i, 0)),
            pl.BlockSpec((None, 6, d), lambda b, i: (b, 0, 0)),
        ],
        out_specs=pl.BlockSpec((None, tm, d), lambda b, i: (b, i, 0)),
        out_shape=jax.ShapeDtypeStruct((bsz, s, d), F32),
        compiler_params=_compiler_params(2),
        name="oproj",
    )(a, w_o, x, mod)


def _ffn_kernel(x_ref, xp_ref, xn_ref, mod_ref, g_ref, wg_ref, wv_ref, cwg_ref, cwv_ref,
                cbg_ref, cbv_ref, wd_ref, gf_ref, o_ref, h_scr, ag_scr, av_scr, *, final_norm, n_chunks):
    tm = x_ref.shape[0]
    halo = xp_ref.shape[0]
    i = pl.program_id(1)
    j = pl.program_id(2)

    rc = tm // n_chunks
    edges = [0] + [c * rc + 2 * halo for c in range(1, n_chunks)] + [tm + 2 * halo]

    def normed_rows(lo, hi):
        gain, scale, shift = g_ref[...], mod_ref[4:5, :], mod_ref[3:4, :]
        parts = []
        if lo == 0:
            hp = _modnorm(xp_ref[...], gain, scale, shift)
            parts.append(jnp.where(i == 0, 0.0, hp))
        x_lo, x_hi = max(lo - halo, 0), min(hi - halo, tm)
        parts.append(_modnorm(x_ref[x_lo:x_hi, :], gain, scale, shift))
        if hi == tm + 2 * halo:
            hn = _modnorm(xn_ref[...], gain, scale, shift)
            parts.append(jnp.where(i == pl.num_programs(1) - 1, 0.0, hn))
        return jnp.concatenate(parts, axis=0).astype(BF16)

    def conv(a_scr, cw_ref, cb_ref, row0):
        base = halo + row0
        return (cw_ref[0:1, :] * a_scr[base - 1:base - 1 + rc, :]
                + cw_ref[1:2, :] * a_scr[base:base + rc, :]
                + cw_ref[2:3, :] * a_scr[base + 1:base + 1 + rc, :]
                + cb_ref[...])

    def step(first):
        for c in range(n_chunks):
            rows = slice(edges[c], edges[c + 1])
            if first:
                h_scr[rows, :] = normed_rows(edges[c], edges[c + 1])
            ag_scr[rows, :] = jnp.dot(h_scr[rows, :], wg_ref[...], preferred_element_type=F32)
            av_scr[rows, :] = jnp.dot(h_scr[rows, :], wv_ref[...], preferred_element_type=F32)
        for c in range(n_chunks):
            act = (_gelu(conv(ag_scr, cwg_ref, cbg_ref, c * rc))
                   * conv(av_scr, cwv_ref, cbv_ref, c * rc)).astype(BF16)
            y = jnp.dot(act, wd_ref[...], preferred_element_type=F32)
            out_rows = slice(c * rc, (c + 1) * rc)
            if first:
                o_ref[out_rows, :] = y
            else:
                o_ref[out_rows, :] += y

    pl.when(j == 0)(lambda: step(True))
    pl.when(j > 0)(lambda: step(False))

    @pl.when(j == pl.num_programs(2) - 1)
    def _():
        y = x_ref[...] + mod_ref[5:6, :] * o_ref[...]
        if final_norm:
            y = (y * _rms_scale(y)) * gf_ref[...]
        o_ref[...] = y


def _ffn(x, mod, g, layer, w_up, conv_w, conv_b, w_down, g_final, final_norm,
         tm=512, tf=512, row_chunks=2):
    bsz, s, d = x.shape
    dff = w_down.shape[1]
    nj = dff // tf
    halo = V7X_SUBLANES
    nhb = tm // halo
    last_hb = s // halo - 1
    kernel = functools.partial(_ffn_kernel, final_norm=final_norm, n_chunks=row_chunks)
    return pl.pallas_call(
        kernel,
        grid=(bsz, s // tm, nj),
        in_specs=[
            pl.BlockSpec((None, tm, d), lambda b, i, j: (b, i, 0)),
            pl.BlockSpec((None, halo, d), lambda b, i, j: (b, jnp.maximum(i * nhb - 1, 0), 0)),
            pl.BlockSpec((None, halo, d), lambda b, i, j: (b, jnp.minimum((i + 1) * nhb, last_hb), 0)),
            pl.BlockSpec((None, 6, d), lambda b, i, j: (b, 0, 0)),
            pl.BlockSpec((1, d), lambda b, i, j: (0, 0)),
            pl.BlockSpec((None, d, tf), lambda b, i, j: (layer, 0, j)),
            pl.BlockSpec((None, d, tf), lambda b, i, j: (layer, 0, nj + j)),
            pl.BlockSpec((None, 3, tf), lambda b, i, j: (layer, 0, j)),
            pl.BlockSpec((None, 3, tf), lambda b, i, j: (layer, 0, nj + j)),
            pl.BlockSpec((None, 1, tf), lambda b, i, j: (layer, 0, j)),
            pl.BlockSpec((None, 1, tf), lambda b, i, j: (layer, 0, nj + j)),
            pl.BlockSpec((None, tf, d), lambda b, i, j: (layer, j, 0)),
            pl.BlockSpec((1, d), lambda b, i, j: (0, 0)),
        ],
        out_specs=pl.BlockSpec((None, tm, d), lambda b, i, j: (b, i, 0)),
        out_shape=jax.ShapeDtypeStruct((bsz, s, d), F32),
        scratch_shapes=[
            pltpu.VMEM((tm + 2 * halo, d), BF16),
            pltpu.VMEM((tm + 2 * halo, tf), F32),
            pltpu.VMEM((tm + 2 * halo, tf), F32),
        ],
        compiler_params=_compiler_params(3),
        name="ffn",
    )(x, x, x, mod, g.reshape(1, d), w_up, w_up, conv_w, conv_w,
      conv_b[:, None, :], conv_b[:, None, :], w_down, g_final.reshape(1, d))


def _rope_tables(s, hd):
    axis_dim = hd // 2
    n_rows = s // GRID_W
    inv_freq = ROPE_THETA ** (-jnp.arange(0, axis_dim, 2, dtype=F32) / axis_dim)
    ang_r = jnp.arange(n_rows).astype(F32)[:, None] * inv_freq[None, :]
    ang_c = jnp.arange(GRID_W).astype(F32)[:, None] * inv_freq[None, :]
    zr, zc = jnp.zeros_like(ang_r), jnp.zeros_like(ang_c)
    cos_r = jnp.concatenate([jnp.cos(ang_r)] * 2 + [zr] * 2, axis=-1)
    sin_r = jnp.concatenate([-jnp.sin(ang_r), jnp.sin(ang_r), zr, zr], axis=-1)
    cos_c = jnp.concatenate([zc] * 2 + [jnp.cos(ang_c)] * 2, axis=-1)
    sin_c = jnp.concatenate([zc, zc, -jnp.sin(ang_c), jnp.sin(ang_c)], axis=-1)
    expand = lambda by_row, by_col: (by_row[:, None, :] + by_col[None, :, :]).reshape(s, hd)
    return expand(cos_r, cos_c), expand(sin_r, sin_c)


def kernel(x, c, w_ada, b_ada, g_norm, g_final, a_w_in, a_g_v, a_w_s, a_b_s, a_w_out,
           b_w_qkv, b_g_q, b_g_k, b_w_o, f_w_up, f_conv_w, f_conv_b, f_w_down):
    depth = w_ada.shape[0]
    bsz, s, d = x.shape
    hd = b_g_q.shape[-1]
    n_heads = d // hd
    n_kv = (b_w_qkv.shape[-1] // hd - n_heads) // 2
    n_mixers = 2

    mod = _adaln(c, w_ada, b_ada)
    w_up_bf16 = f_w_up.astype(BF16)
    w_down_bf16 = f_w_down.astype(BF16)
    cos, sin = _rope_tables(s, hd)

    for i in range(depth):
        j = i // n_mixers
        if i % n_mixers == 0:
            n_groups, chunk, _ = a_w_s.shape[1:]
            width = a_w_out.shape[1]
            bs_full = jnp.repeat(jnp.transpose(a_b_s[j]), width // n_groups, axis=1)
            u, vn = _gmlp_in(x, mod[i], g_norm[i, 0], a_w_in[j].astype(BF16), a_g_v[j])
            x = _gmlp_out(u, vn, a_w_s[j].astype(BF16), bs_full, x, mod[i], a_w_out[j].astype(BF16))
        else:
            w = b_w_qkv[j]
            qk_cols = (n_heads + n_kv) * hd
            q, k, vt = _qkv(x, mod[i], g_norm[i, 0], w.astype(BF16),
                            jnp.transpose(w[:, qk_cols:]).astype(BF16), b_g_q[j], b_g_k[j],
                            cos, sin, n_kv)
            score_bound = (1.01 * math.log2(math.e) * math.sqrt(hd)
                           * jnp.max(jnp.abs(b_g_q[j])) * jnp.max(jnp.abs(b_g_k[j])))
            a = _attention(q, k, vt, score_bound)
            x = _oproj(a, b_w_o[j].astype(BF16), x, mod[i])
        x = _ffn(x, mod[i], g_norm[i, 1], i, w_up_bf16, f_conv_w, f_conv_b, w_down_bf16,
                 g_final, final_norm=(i == depth - 1))
    return x
```

```python
import functools
import math

import jax
import jax.numpy as jnp
from jax import lax
from jax.experimental import pallas as pl
from jax.experimental.pallas import tpu as pltpu

EPS = 1e-6
GRID_W = 64
ROPE_THETA = 10000.0
V7X_VMEM_BYTES = 64 * 1024 * 1024
V7X_SUBLANES = 8
V7X_BF16_SUBLANES = 16
VMEM_LIMIT_BYTES = V7X_VMEM_BYTES - 8 * 1024 * 1024

BF16 = jnp.bfloat16
F32 = jnp.float32


def _compiler_params(n_grid_axes):
    return pltpu.CompilerParams(
        dimension_semantics=("arbitrary",) * n_grid_axes,
        vmem_limit_bytes=VMEM_LIMIT_BYTES,
    )


def _rms_scale(x):
    return lax.rsqrt(jnp.mean(x * x, axis=-1, keepdims=True) + EPS)


def _modnorm(x, gain, scale, shift):
    return (x * _rms_scale(x)) * (gain * (1.0 + scale)) + shift


def _gelu(x):
    return 0.5 * x * (1.0 + lax.erf(x * (1.0 / math.sqrt(2.0))))


def _adaln_kernel(c_ref, w_ref, b_ref, o_ref):
    c = c_ref[...]
    cond = c * (1.0 / (1.0 + jnp.exp(-c)))
    o_ref[...] = jnp.dot(cond, w_ref[...], precision=lax.Precision.HIGHEST,
                         preferred_element_type=F32) + b_ref[...]


def _adaln(c, w_ada, b_ada, tn=1024):
    depth, d, n = w_ada.shape
    bsz = c.shape[0]
    rows = -(-bsz // V7X_SUBLANES) * V7X_SUBLANES
    c_pad = jnp.zeros((rows, d), F32).at[:bsz].set(c)
    out = pl.pallas_call(
        _adaln_kernel,
        grid=(depth, n // tn),
        in_specs=[
            pl.BlockSpec((rows, d), lambda l, j: (0, 0)),
            pl.BlockSpec((None, d, tn), lambda l, j: (l, 0, j)),
            pl.BlockSpec((None, 1, tn), lambda l, j: (l, 0, j)),
        ],
        out_specs=pl.BlockSpec((None, rows, tn), lambda l, j: (l, 0, j)),
        out_shape=jax.ShapeDtypeStruct((depth, rows, n), F32),
        compiler_params=_compiler_params(2),
        name="adaln",
    )(c_pad, w_ada, b_ada.reshape(depth, 1, n))
    return out[:, :bsz].reshape(depth, bsz, 6, d)


def _gmlp_in_kernel(x_ref, mod_ref, g_ref, w_ref, gv_ref, u_ref, vn_ref, h_scr, v_scr, *, nc):
    width = u_ref.shape[-1]
    h_scr[...] = _modnorm(x_ref[...], g_ref[...], mod_ref[1:2, :], mod_ref[0:1, :]).astype(BF16)
    ssq = jnp.zeros((x_ref.shape[0], 1), F32)
    for c in range(width // nc):
        cols = slice(c * nc, (c + 1) * nc)
        zu = _gelu(jnp.dot(h_scr[...], w_ref[:, cols], preferred_element_type=F32))
        u_ref[:, cols] = zu.astype(BF16)
        zv = _gelu(jnp.dot(h_scr[...], w_ref[:, width + c * nc: width + (c + 1) * nc],
                           preferred_element_type=F32))
        v_scr[:, cols] = zv
        ssq = ssq + jnp.sum(zv * zv, axis=-1, keepdims=True)
    r = lax.rsqrt(ssq * (1.0 / width) + EPS)
    vn_ref[...] = ((v_scr[...] * r) * gv_ref[...]).astype(BF16)


def _gmlp_in(x, mod, g, w_in, g_v, tm=512, nc=512):
    bsz, s, d = x.shape
    width = w_in.shape[1] // 2
    kernel = functools.partial(_gmlp_in_kernel, nc=nc)
    return pl.pallas_call(
        kernel,
        grid=(bsz, s // tm),
        in_specs=[
            pl.BlockSpec((None, tm, d), lambda b, i: (b, i, 0)),
            pl.BlockSpec((None, 6, d), lambda b, i: (b, 0, 0)),
            pl.BlockSpec((1, d), lambda b, i: (0, 0)),
            pl.BlockSpec((d, 2 * width), lambda b, i: (0, 0)),
            pl.BlockSpec((1, width), lambda b, i: (0, 0)),
        ],
        out_specs=[
            pl.BlockSpec((None, tm, width), lambda b, i: (b, i, 0)),
            pl.BlockSpec((None, tm, width), lambda b, i: (b, i, 0)),
        ],
        out_shape=[jax.ShapeDtypeStruct((bsz, s, width), BF16)] * 2,
        scratch_shapes=[pltpu.VMEM((tm, d), BF16), pltpu.VMEM((tm, width), F32)],
        compiler_params=_compiler_params(2),
        name="gmlp_in",
    )(x, mod, g.reshape(1, d), w_in, g_v.reshape(1, width))


def _gmlp_out_kernel(u_ref, vn_ref, ws_ref, bs_ref, x_ref, mod_ref, wo_ref, o_ref, y_scr):
    n_groups, chunk, _ = ws_ref.shape
    gd = u_ref.shape[-1] // n_groups
    for n in range(u_ref.shape[0] // chunk):
        rows = slice(n * chunk, (n + 1) * chunk)
        for g in range(n_groups):
            cols = slice(g * gd, (g + 1) * gd)
            sv = jnp.dot(ws_ref[g], vn_ref[rows, cols], preferred_element_type=F32) + bs_ref[:, cols]
            y_scr[rows, cols] = (u_ref[rows, cols].astype(F32) * sv).astype(BF16)
    y = jnp.dot(y_scr[...], wo_ref[...], preferred_element_type=F32)
    o_ref[...] = x_ref[...] + mod_ref[2:3, :] * y


def _gmlp_out(u, vn, w_s, bs_full, x, mod, w_out, tm=512):
    bsz, s, d = x.shape
    width = u.shape[-1]
    n_groups, chunk, _ = w_s.shape
    return pl.pallas_call(
        _gmlp_out_kernel,
        grid=(bsz, s // tm),
        in_specs=[
            pl.BlockSpec((None, tm, width), lambda b, i: (b, i, 0)),
            pl.BlockSpec((None, tm, width), lambda b, i: (b, i, 0)),
            pl.BlockSpec((n_groups, chunk, chunk), lambda b, i: (0, 0, 0)),
            pl.BlockSpec((chunk, width), lambda b, i: (0, 0)),
            pl.BlockSpec((None, tm, d), lambda b, i: (b, i, 0)),
            pl.BlockSpec((None, 6, d), lambda b, i: (b, 0, 0)),
            pl.BlockSpec((width, d), lambda b, i: (0, 0)),
        ],
        out_specs=pl.BlockSpec((None, tm, d), lambda b, i: (b, i, 0)),
        out_shape=jax.ShapeDtypeStruct((bsz, s, d), F32),
        scratch_shapes=[pltpu.VMEM((tm, width), BF16)],
        compiler_params=_compiler_params(2),
        name="gmlp_out",
    )(u, vn, w_s, bs_full, x, mod, w_out)


def _rope(t, cos, sin_signed, half_idx):
    quarter = t.shape[-1] // 4
    partner = jnp.where(half_idx, pltpu.roll(t, t.shape[-1] - quarter, 1), pltpu.roll(t, quarter, 1))
    return t * cos + partner * sin_signed


def _qkv_kernel(x_ref, mod_ref, g_ref, wqk_ref, wvt_ref, gq_ref, gk_ref, cos_ref, sin_ref,
                q_ref, k_ref, vt_ref, *, q_scale, row_chunk):
    n_kv, group, tm, hd = q_ref.shape
    lane = lax.broadcasted_iota(jnp.int32, (row_chunk, hd), 1)
    half_idx = (lane % (hd // 2)) < (hd // 4)
    pad_rows = vt_ref.shape[2] - hd
    ones_row = (lax.broadcasted_iota(jnp.int32, (pad_rows, row_chunk), 0) == 0).astype(BF16)

    for r0 in range(0, tm, row_chunk):
        rows = slice(r0, r0 + row_chunk)
        h = _modnorm(x_ref[rows, :], g_ref[...], mod_ref[1:2, :], mod_ref[0:1, :]).astype(BF16)
        cos = cos_ref[rows, :]
        sin = sin_ref[rows, :]

        def head(t, gain, post):
            t = t * lax.rsqrt(jnp.mean(t * t, axis=-1, keepdims=True) + EPS) * gain
            return (_rope(t, cos, sin, half_idx) * post).astype(BF16)

        for kv in range(n_kv):
            cols = slice(kv * group * hd, (kv + 1) * group * hd)
            tq = jnp.dot(h, wqk_ref[:, cols], preferred_element_type=F32)
            for g in range(group):
                q_ref[kv, g, rows, :] = head(tq[:, g * hd:(g + 1) * hd], gq_ref[...], q_scale)
        k_off = n_kv * group * hd
        tk = jnp.dot(h, wqk_ref[:, k_off:k_off + n_kv * hd], preferred_element_type=F32)
        for kv in range(n_kv):
            k_ref[kv, rows, :] = head(tk[:, kv * hd:(kv + 1) * hd], gk_ref[...], 1.0)
        vt = lax.dot_general(wvt_ref[...], h, (((1,), (1,)), ((), ())),
                             preferred_element_type=F32)
        for kv in range(n_kv):
            vt_ref[kv, 0, :hd, rows] = vt[kv * hd:(kv + 1) * hd, :].astype(BF16)
            vt_ref[kv, 0, hd:, rows] = ones_row


def _qkv(x, mod, g, w_qk, w_vt, g_q, g_k, cos, sin, n_kv, tm=512, row_chunk=256):
    bsz, s, d = x.shape
    hd = g_q.shape[-1]
    n_heads = d // hd
    group = n_heads // n_kv
    kernel = functools.partial(_qkv_kernel, q_scale=math.log2(math.e) / math.sqrt(hd),
                               row_chunk=row_chunk)
    return pl.pallas_call(
        kernel,
        grid=(bsz, s // tm),
        in_specs=[
            pl.BlockSpec((None, tm, d), lambda b, i: (b, i, 0)),
            pl.BlockSpec((None, 6, d), lambda b, i: (b, 0, 0)),
            pl.BlockSpec((1, d), lambda b, i: (0, 0)),
            pl.BlockSpec(w_qk.shape, lambda b, i: (0, 0)),
            pl.BlockSpec(w_vt.shape, lambda b, i: (0, 0)),
            pl.BlockSpec((1, hd), lambda b, i: (0, 0)),
            pl.BlockSpec((1, hd), lambda b, i: (0, 0)),
            pl.BlockSpec((tm, hd), lambda b, i: (i, 0)),
            pl.BlockSpec((tm, hd), lambda b, i: (i, 0)),
        ],
        out_specs=[
            pl.BlockSpec((None, n_kv, group, tm, hd), lambda b, i: (b, 0, 0, i, 0)),
            pl.BlockSpec((None, n_kv, tm, hd), lambda b, i: (b, 0, i, 0)),
            pl.BlockSpec((None, n_kv, 1, hd + V7X_BF16_SUBLANES, tm), lambda b, i: (b, 0, i, 0, 0)),
        ],
        out_shape=[
            jax.ShapeDtypeStruct((bsz, n_kv, group, s, hd), BF16),
            jax.ShapeDtypeStruct((bsz, n_kv, s, hd), BF16),
            jax.ShapeDtypeStruct((bsz, n_kv, s // tm, hd + V7X_BF16_SUBLANES, tm), BF16),
        ],
        compiler_params=_compiler_params(2),
        name="qkv",
    )(x, mod, g.reshape(1, d), w_qk, w_vt, g_q.reshape(1, hd), g_k.reshape(1, hd), cos, sin)


ATTN_SLOTS = 4


def _attn_kernel(q_ref, k_ref, vt_ref, o_ref, acc_scr, *slots):
    group, tq, hd = q_ref.shape
    n_kt, _, tk = vt_ref.shape
    nq = group * tq
    s_scr, p_scr = slots[:ATTN_SLOTS], slots[ATTN_SLOTS:]
    st_idx = jnp.minimum(pl.program_id(2), 0)
    ld_idx = jnp.minimum(pl.program_id(1), 0)

    def scores(kt, s_scr):
        k_tile = k_ref[pl.ds(pl.multiple_of(kt * tk, tk), tk), :]
        s = lax.dot_general(k_tile, q_ref[...].reshape(nq, hd), (((1,), (1,)), ((), ())),
                            preferred_element_type=F32)
        s_scr[st_idx] = s
        return jnp.max(s, axis=0, keepdims=True)

    def softmax(s_scr, p_scr, m, tile_max):
        m_new = jnp.maximum(m, tile_max)
        p_scr[st_idx] = jnp.exp2(s_scr[ld_idx] - m_new).astype(BF16)
        return m_new, jnp.exp2(m - m_new)

    def values(kt, p_scr, alpha):
        pv = jnp.dot(vt_ref[kt], p_scr[ld_idx], preferred_element_type=F32)
        acc_scr[...] = alpha * acc_scr[...] + pv

    def tile(t, slot, carry, prefetch):
        m, max_t, max_t1, alpha_t2, alpha_t1 = carry
        max_t2 = scores(t + 2, s_scr[(slot + 2) % ATTN_SLOTS]) if prefetch else None
        if alpha_t2 is not None:
            values(t - 2, p_scr[(slot - 2) % ATTN_SLOTS], alpha_t2)
        m, alpha = softmax(s_scr[slot], p_scr[slot], m, max_t)
        return m, max_t1, max_t2, alpha_t1, alpha

    def group_of_tiles(t0, carry, n_prefetch=ATTN_SLOTS):
        for slot in range(ATTN_SLOTS):
            carry = tile(t0 + slot, slot, carry, prefetch=slot < n_prefetch)
        return carry

    n_groups = n_kt // ATTN_SLOTS
    acc_scr[...] = jnp.zeros_like(acc_scr)
    m0 = jnp.full((1, nq), -jnp.inf, F32)
    carry = (m0, scores(0, s_scr[0]), scores(1, s_scr[1]), None, None)
    carry = group_of_tiles(0, carry)
    carry = lax.fori_loop(1, n_groups - 1, lambda g, c: group_of_tiles(g * ATTN_SLOTS, c), carry,
                          unroll=True)
    _, _, _, alpha_t2, alpha_t1 = group_of_tiles(n_kt - ATTN_SLOTS, carry,
                                                 n_prefetch=ATTN_SLOTS - 2)
    values(n_kt - 2, p_scr[ATTN_SLOTS - 2], alpha_t2)
    values(n_kt - 1, p_scr[ATTN_SLOTS - 1], alpha_t1)
    out_t = acc_scr[:hd, :] / acc_scr[hd:hd + 1, :]
    for g in range(group):
        o_ref[:, g * hd:(g + 1) * hd] = out_t[:, g * tq:(g + 1) * tq].T.astype(BF16)


def _attn_bounded_kernel(q_ref, k_ref, vt_ref, o_ref, acc_scr, *p_scr):
    group, tq_step, hd = q_ref.shape
    n_kt, _, tk = vt_ref.shape
    n_blocks = acc_scr.shape[0]
    tq = tq_step // n_blocks
    nq = group * tq
    st_idx = jnp.minimum(pl.program_id(2), 0)
    ld_idx = jnp.minimum(pl.program_id(1), 0)

    for blk in range(n_blocks):
        q_rows = slice(blk * tq, (blk + 1) * tq)
        acc = acc_scr.at[blk]

        def probabilities(kt):
            k_tile = k_ref[kt * tk:(kt + 1) * tk, :]
            s = lax.dot_general(k_tile, q_ref[:, q_rows, :].reshape(nq, hd),
                                (((1,), (1,)), ((), ())), preferred_element_type=F32)
            p = jnp.exp2(s)
            p_scr[kt % ATTN_SLOTS][st_idx] = p.astype(BF16)
            return jnp.sum(p, axis=0, keepdims=True)

        def values(kt):
            acc[...] += jnp.dot(vt_ref[kt, :hd, :], p_scr[kt % ATTN_SLOTS][ld_idx],
                                preferred_element_type=F32)

        acc[...] = jnp.zeros_like(acc)
        l = probabilities(0) + probabilities(1)
        for kt in range(n_kt):
            if kt + 2 < n_kt:
                l = l + probabilities(kt + 2)
            values(kt)
        out_t = acc[...] / l
        for g in range(group):
            o_ref[q_rows, g * hd:(g + 1) * hd] = out_t[:, g * tq:(g + 1) * tq].T.astype(BF16)


MAX_UNSTABILISED_SCORE = 60.0


def _attention(q, k, vt, score_bound, tq=128, bounded_blocks=4):
    bsz, n_kv, group, s, hd = q.shape
    n_kt, tk = vt.shape[2], vt.shape[4]
    assert n_kt % ATTN_SLOTS == 0 and n_kt >= 2 * ATTN_SLOTS
    nq = group * tq

    def call(body, scratch, name, tq_step):
        return pl.pallas_call(
            body,
            grid=(bsz, n_kv, s // tq_step),
            in_specs=[
                pl.BlockSpec((None, None, group, tq_step, hd), lambda b, h, i: (b, h, 0, i, 0)),
                pl.BlockSpec((None, None, s, hd), lambda b, h, i: (b, h, 0, 0)),
                pl.BlockSpec((None, None, n_kt, vt.shape[3], tk), lambda b, h, i: (b, h, 0, 0, 0)),
            ],
            out_specs=pl.BlockSpec((None, tq_step, group * hd), lambda b, h, i: (b, i, h)),
            out_shape=jax.ShapeDtypeStruct((bsz, s, n_kv * group * hd), BF16),
            scratch_shapes=scratch,
            compiler_params=_compiler_params(3),
            name=name,
        )(q, k, vt)

    p_slots = [pltpu.VMEM((1, tk, nq), BF16)] * ATTN_SLOTS
    s_slots = [pltpu.VMEM((1, tk, nq), F32)] * ATTN_SLOTS
    return lax.cond(
        score_bound <= MAX_UNSTABILISED_SCORE,
        lambda: call(_attn_bounded_kernel, [pltpu.VMEM((bounded_blocks, hd, nq), F32)] + p_slots,
                     "attention_bounded", bounded_blocks * tq),
        lambda: call(_attn_kernel, [pltpu.VMEM((vt.shape[3], nq), F32)] + s_slots + p_slots,
                     "attention", tq),
    )


def _oproj_kernel(a_ref, w_ref, x_ref, mod_ref, o_ref):
    y = jnp.dot(a_ref[...], w_ref[...], preferred_element_type=F32)
    o_ref[...] = x_ref[...] + mod_ref[2:3, :] * y


def _oproj(a, w_o, x, mod, tm=512):
    bsz, s, d = x.shape
    return pl.pallas_call(
        _oproj_kernel,
        grid=(bsz, s // tm),
        in_specs=[
            pl.BlockSpec((None, tm, a.shape[-1]), lambda b, i: (b, i, 0)),
            pl.BlockSpec(w_o.shape, lambda b, i: (0, 0)),
            pl.BlockSpec((None, tm, d), lambda b, i: (b, i, 0)),
            pl.BlockSpec((None, 6, d), lambda b, i: (b, 0, 0)),
        ],
        out_specs=pl.BlockSpec((None, tm, d), lambda b, i: (b, i, 0)),
        out_shape=jax.ShapeDtypeStruct((bsz, s, d), F32),
        compiler_params=_compiler_params(2),
        name="oproj",
    )(a, w_o, x, mod)


def _ffn_kernel(x_ref, xp_ref, xn_ref, mod_ref, g_ref, wg_ref, wv_ref, cwg_ref, cwv_ref,
                cbg_ref, cbv_ref, wd_ref, gf_ref, o_ref, h_scr, ag_scr, av_scr, *, final_norm, n_chunks):
    tm = x_ref.shape[0]
    halo = xp_ref.shape[0]
    i = pl.program_id(1)
    j = pl.program_id(2)

    rc = tm // n_chunks
    edges = [0] + [c * rc + 2 * halo for c in range(1, n_chunks)] + [tm + 2 * halo]

    def normed_rows(lo, hi):
        gain, scale, shift = g_ref[...], mod_ref[4:5, :], mod_ref[3:4, :]
        parts = []
        if lo == 0:
            hp = _modnorm(xp_ref[...], gain, scale, shift)
            parts.append(jnp.where(i == 0, 0.0, hp))
        x_lo, x_hi = max(lo - halo, 0), min(hi - halo, tm)
        parts.append(_modnorm(x_ref[x_lo:x_hi, :], gain, scale, shift))
        if hi == tm + 2 * halo:
            hn = _modnorm(xn_ref[...], gain, scale, shift)
            parts.append(jnp.where(i == pl.num_programs(1) - 1, 0.0, hn))
        return jnp.concatenate(parts, axis=0).astype(BF16)

    def conv(a_scr, cw_ref, cb_ref, row0):
        base = halo + row0
        return (cw_ref[0:1, :] * a_scr[base - 1:base - 1 + rc, :]
                + cw_ref[1:2, :] * a_scr[base:base + rc, :]
                + cw_ref[2:3, :] * a_scr[base + 1:base + 1 + rc, :]
                + cb_ref[...])

    def step(first, last):
        for c in range(n_chunks):
            rows = slice(edges[c], edges[c + 1])
            if first:
                h_scr[rows, :] = normed_rows(edges[c], edges[c + 1])
            ag_scr[rows, :] = jnp.dot(h_scr[rows, :], wg_ref[...], preferred_element_type=F32)
            av_scr[rows, :] = jnp.dot(h_scr[rows, :], wv_ref[...], preferred_element_type=F32)
        for c in range(n_chunks):
            act = (_gelu(conv(ag_scr, cwg_ref, cbg_ref, c * rc))
                   * conv(av_scr, cwv_ref, cbv_ref, c * rc)).astype(BF16)
            y = jnp.dot(act, wd_ref[...], preferred_element_type=F32)
            out_rows = slice(c * rc, (c + 1) * rc)
            if not first:
                y = o_ref[out_rows, :] + y
            if last:
                y = x_ref[out_rows, :] + mod_ref[5:6, :] * y
                if final_norm:
                    y = (y * _rms_scale(y)) * gf_ref[...]
            o_ref[out_rows, :] = y

    n_j = pl.num_programs(2)
    pl.when(j == 0)(lambda: step(True, False))
    pl.when(jnp.logical_and(j > 0, j < n_j - 1))(lambda: step(False, False))
    pl.when(j == n_j - 1)(lambda: step(False, True))


def _ffn(x, mod, g, layer, w_up, conv_w, conv_b, w_down, g_final, final_norm,
         tm=512, tf=512, row_chunks=2):
    bsz, s, d = x.shape
    dff = w_down.shape[1]
    nj = dff // tf
    assert nj >= 2
    halo = V7X_SUBLANES
    nhb = tm // halo
    last_hb = s // halo - 1
    kernel = functools.partial(_ffn_kernel, final_norm=final_norm, n_chunks=row_chunks)
    return pl.pallas_call(
        kernel,
        grid=(bsz, s // tm, nj),
        in_specs=[
            pl.BlockSpec((None, tm, d), lambda b, i, j: (b, i, 0)),
            pl.BlockSpec((None, halo, d), lambda b, i, j: (b, jnp.maximum(i * nhb - 1, 0), 0)),
            pl.BlockSpec((None, halo, d), lambda b, i, j: (b, jnp.minimum((i + 1) * nhb, last_hb), 0)),
            pl.BlockSpec((None, 6, d), lambda b, i, j: (b, 0, 0)),
            pl.BlockSpec((1, d), lambda b, i, j: (0, 0)),
            pl.BlockSpec((None, d, tf), lambda b, i, j: (layer, 0, j)),
            pl.BlockSpec((None, d, tf), lambda b, i, j: (layer, 0, nj + j)),
            pl.BlockSpec((None, 3, tf), lambda b, i, j: (layer, 0, j)),
            pl.BlockSpec((None, 3, tf), lambda b, i, j: (layer, 0, nj + j)),
            pl.BlockSpec((None, 1, tf), lambda b, i, j: (layer, 0, j)),
            pl.BlockSpec((None, 1, tf), lambda b, i, j: (layer, 0, nj + j)),
            pl.BlockSpec((None, tf, d), lambda b, i, j: (layer, j, 0)),
            pl.BlockSpec((1, d), lambda b, i, j: (0, 0)),
        ],
        out_specs=pl.BlockSpec((None, tm, d), lambda b, i, j: (b, i, 0)),
        out_shape=jax.ShapeDtypeStruct((bsz, s, d), F32),
        scratch_shapes=[
            pltpu.VMEM((tm + 2 * halo, d), BF16),
            pltpu.VMEM((tm + 2 * halo, tf), F32),
            pltpu.VMEM((tm + 2 * halo, tf), F32),
        ],
        compiler_params=_compiler_params(3),
        name="ffn",
    )(x, x, x, mod, g.reshape(1, d), w_up, w_up, conv_w, conv_w,
      conv_b[:, None, :], conv_b[:, None, :], w_down, g_final.reshape(1, d))


def _rope_tables(s, hd):
    axis_dim = hd // 2
    n_rows = s // GRID_W
    inv_freq = ROPE_THETA ** (-jnp.arange(0, axis_dim, 2, dtype=F32) / axis_dim)
    ang_r = jnp.arange(n_rows).astype(F32)[:, None] * inv_freq[None, :]
    ang_c = jnp.arange(GRID_W).astype(F32)[:, None] * inv_freq[None, :]
    zr, zc = jnp.zeros_like(ang_r), jnp.zeros_like(ang_c)
    cos_r = jnp.concatenate([jnp.cos(ang_r)] * 2 + [zr] * 2, axis=-1)
    sin_r = jnp.concatenate([-jnp.sin(ang_r), jnp.sin(ang_r), zr, zr], axis=-1)
    cos_c = jnp.concatenate([zc] * 2 + [jnp.cos(ang_c)] * 2, axis=-1)
    sin_c = jnp.concatenate([zc, zc, -jnp.sin(ang_c), jnp.sin(ang_c)], axis=-1)
    expand = lambda by_row, by_col: (by_row[:, None, :] + by_col[None, :, :]).reshape(s, hd)
    return expand(cos_r, cos_c), expand(sin_r, sin_c)


def kernel(x, c, w_ada, b_ada, g_norm, g_final, a_w_in, a_g_v, a_w_s, a_b_s, a_w_out,
           b_w_qkv, b_g_q, b_g_k, b_w_o, f_w_up, f_conv_w, f_conv_b, f_w_down):
    depth = w_ada.shape[0]
    bsz, s, d = x.shape
    hd = b_g_q.shape[-1]
    n_heads = d // hd
    n_kv = (b_w_qkv.shape[-1] // hd - n_heads) // 2
    n_mixers = 2

    mod = _adaln(c, w_ada, b_ada)
    w_up_bf16 = f_w_up.astype(BF16)
    w_down_bf16 = f_w_down.astype(BF16)
    cos, sin = _rope_tables(s, hd)

    for i in range(depth):
        j = i // n_mixers
        if i % n_mixers == 0:
            n_groups, chunk, _ = a_w_s.shape[1:]
            width = a_w_out.shape[1]
            bs_full = jnp.repeat(jnp.transpose(a_b_s[j]), width // n_groups, axis=1)
            u, vn = _gmlp_in(x, mod[i], g_norm[i, 0], a_w_in[j].astype(BF16), a_g_v[j])
            x = _gmlp_out(u, vn, a_w_s[j].astype(BF16), bs_full, x, mod[i], a_w_out[j].astype(BF16))
        else:
            w = b_w_qkv[j]
            qk_cols = (n_heads + n_kv) * hd
            q, k, vt = _qkv(x, mod[i], g_norm[i, 0], w.astype(BF16),
                            jnp.transpose(w[:, qk_cols:]).astype(BF16), b_g_q[j], b_g_k[j],
                            cos, sin, n_kv)
            score_bound = (1.01 * math.log2(math.e) * math.sqrt(hd)
                           * jnp.max(jnp.abs(b_g_q[j])) * jnp.max(jnp.abs(b_g_k[j])))
            a = _attention(q, k, vt, score_bound)
            x = _oproj(a, b_w_o[j].astype(BF16), x, mod[i])
        x = _ffn(x, mod[i], g_norm[i, 1], i, w_up_bf16, f_conv_w, f_conv_b, w_down_bf16,
                 g_final, final_norm=(i == depth - 1))
    return x
```

```python
import functools
import math

import jax
import jax.numpy as jnp
from jax import lax
from jax.experimental import pallas as pl
from jax.experimental.pallas import tpu as pltpu

EPS = 1e-6
GRID_W = 64
ROPE_THETA = 10000.0
V7X_VMEM_BYTES = 64 * 1024 * 1024
V7X_SUBLANES = 8
V7X_BF16_SUBLANES = 16
VMEM_LIMIT_BYTES = V7X_VMEM_BYTES - 8 * 1024 * 1024

BF16 = jnp.bfloat16
F32 = jnp.float32


def _compiler_params(n_grid_axes):
    return pltpu.CompilerParams(
        dimension_semantics=("arbitrary",) * n_grid_axes,
        vmem_limit_bytes=VMEM_LIMIT_BYTES,
    )


def _rms_scale(x):
    return lax.rsqrt(jnp.mean(x * x, axis=-1, keepdims=True) + EPS)


def _modnorm(x, gain, scale, shift):
    return (x * _rms_scale(x)) * (gain * (1.0 + scale)) + shift


def _gelu(x):
    return 0.5 * x * (1.0 + lax.erf(x * (1.0 / math.sqrt(2.0))))


def _adaln_kernel(c_ref, w_ref, b_ref, o_ref):
    c = c_ref[...]
    cond = c * (1.0 / (1.0 + jnp.exp(-c)))
    o_ref[...] = jnp.dot(cond, w_ref[...], precision=lax.Precision.HIGHEST,
                         preferred_element_type=F32) + b_ref[...]


def _adaln(c, w_ada, b_ada, tn=1024):
    depth, d, n = w_ada.shape
    bsz = c.shape[0]
    rows = -(-bsz // V7X_SUBLANES) * V7X_SUBLANES
    c_pad = jnp.zeros((rows, d), F32).at[:bsz].set(c)
    out = pl.pallas_call(
        _adaln_kernel,
        grid=(depth, n // tn),
        in_specs=[
            pl.BlockSpec((rows, d), lambda l, j: (0, 0)),
            pl.BlockSpec((None, d, tn), lambda l, j: (l, 0, j)),
            pl.BlockSpec((None, 1, tn), lambda l, j: (l, 0, j)),
        ],
        out_specs=pl.BlockSpec((None, rows, tn), lambda l, j: (l, 0, j)),
        out_shape=jax.ShapeDtypeStruct((depth, rows, n), F32),
        compiler_params=_compiler_params(2),
        name="adaln",
    )(c_pad, w_ada, b_ada.reshape(depth, 1, n))
    return out[:, :bsz].reshape(depth, bsz, 6, d)


def _gmlp_kernel(x_ref, mod_ref, g_ref, win_ref, gv_ref, ws_ref, bs_ref, wo_ref, o_ref,
                 u_scr, v_scr, vn_scr, y_scr, *, nc):
    width = u_scr.shape[-1]
    n_groups, chunk, _ = ws_ref.shape
    gd = width // n_groups
    x = x_ref[...]
    h = _modnorm(x, g_ref[...], mod_ref[1:2, :], mod_ref[0:1, :]).astype(BF16)
    ssq = jnp.zeros((x.shape[0], 1), F32)
    for c in range(width // nc):
        cols = slice(c * nc, (c + 1) * nc)
        u_scr[:, cols] = _gelu(jnp.dot(h, win_ref[:, cols], preferred_element_type=F32))
        zv = _gelu(jnp.dot(h, win_ref[:, width + c * nc: width + (c + 1) * nc],
                           preferred_element_type=F32))
        v_scr[:, cols] = zv
        ssq = ssq + jnp.sum(zv * zv, axis=-1, keepdims=True)
    r = lax.rsqrt(ssq * (1.0 / width) + EPS)
    vn_scr[...] = ((v_scr[...] * r) * gv_ref[...]).astype(BF16)
    for n in range(x.shape[0] // chunk):
        rows = slice(n * chunk, (n + 1) * chunk)
        for g in range(n_groups):
            cols = slice(g * gd, (g + 1) * gd)
            sv = jnp.dot(ws_ref[g], vn_scr[rows, cols], preferred_element_type=F32) + bs_ref[:, cols]
            y_scr[rows, cols] = (u_scr[rows, cols] * sv).astype(BF16)
    y = jnp.dot(y_scr[...], wo_ref[...], preferred_element_type=F32)
    o_ref[...] = x + mod_ref[2:3, :] * y


def _gmlp(x, mod, g, w_in, g_v, w_s, bs_full, w_out, tm=256, nc=512):
    bsz, s, d = x.shape
    width = w_in.shape[1] // 2
    n_groups, chunk, _ = w_s.shape
    return pl.pallas_call(
        functools.partial(_gmlp_kernel, nc=nc),
        grid=(bsz, s // tm),
        in_specs=[
            pl.BlockSpec((None, tm, d), lambda b, i: (b, i, 0)),
            pl.BlockSpec((None, 6, d), lambda b, i: (b, 0, 0)),
            pl.BlockSpec((1, d), lambda b, i: (0, 0)),
            pl.BlockSpec((d, 2 * width), lambda b, i: (0, 0)),
            pl.BlockSpec((1, width), lambda b, i: (0, 0)),
            pl.BlockSpec((n_groups, chunk, chunk), lambda b, i: (0, 0, 0)),
            pl.BlockSpec((chunk, width), lambda b, i: (0, 0)),
            pl.BlockSpec((width, d), lambda b, i: (0, 0)),
        ],
        out_specs=pl.BlockSpec((None, tm, d), lambda b, i: (b, i, 0)),
        out_shape=jax.ShapeDtypeStruct((bsz, s, d), F32),
        scratch_shapes=[pltpu.VMEM((tm, width), F32), pltpu.VMEM((tm, width), F32),
                        pltpu.VMEM((tm, width), BF16), pltpu.VMEM((tm, width), BF16)],
        compiler_params=_compiler_params(2),
        name="gmlp",
    )(x, mod, g.reshape(1, d), w_in, g_v.reshape(1, width), w_s, bs_full, w_out)


def _rope(t, cos, sin_signed, half_idx):
    quarter = t.shape[-1] // 4
    partner = jnp.where(half_idx, pltpu.roll(t, t.shape[-1] - quarter, 1), pltpu.roll(t, quarter, 1))
    return t * cos + partner * sin_signed


def _qkv_kernel(x_ref, mod_ref, g_ref, wqk_ref, wvt_ref, gq_ref, gk_ref, cos_ref, sin_ref,
                q_ref, k_ref, vt_ref, *, q_scale, row_chunk):
    n_kv, group, tm, hd = q_ref.shape
    lane = lax.broadcasted_iota(jnp.int32, (row_chunk, hd), 1)
    half_idx = (lane % (hd // 2)) < (hd // 4)
    pad_rows = vt_ref.shape[2] - hd
    ones_row = (lax.broadcasted_iota(jnp.int32, (pad_rows, row_chunk), 0) == 0).astype(BF16)

    for r0 in range(0, tm, row_chunk):
        rows = slice(r0, r0 + row_chunk)
        h = _modnorm(x_ref[rows, :], g_ref[...], mod_ref[1:2, :], mod_ref[0:1, :]).astype(BF16)
        cos = cos_ref[rows, :]
        sin = sin_ref[rows, :]

        def head(t, gain, post):
            t = t * lax.rsqrt(jnp.mean(t * t, axis=-1, keepdims=True) + EPS) * gain
            return (_rope(t, cos, sin, half_idx) * post).astype(BF16)

        for kv in range(n_kv):
            cols = slice(kv * group * hd, (kv + 1) * group * hd)
            tq = jnp.dot(h, wqk_ref[:, cols], preferred_element_type=F32)
            for g in range(group):
                q_ref[kv, g, rows, :] = head(tq[:, g * hd:(g + 1) * hd], gq_ref[...], q_scale)
        k_off = n_kv * group * hd
        tk = jnp.dot(h, wqk_ref[:, k_off:k_off + n_kv * hd], preferred_element_type=F32)
        for kv in range(n_kv):
            k_ref[kv, rows, :] = head(tk[:, kv * hd:(kv + 1) * hd], gk_ref[...], 1.0)
        vt = lax.dot_general(wvt_ref[...], h, (((1,), (1,)), ((), ())),
                             preferred_element_type=F32)
        for kv in range(n_kv):
            vt_ref[kv, 0, :hd, rows] = vt[kv * hd:(kv + 1) * hd, :].astype(BF16)
            vt_ref[kv, 0, hd:, rows] = ones_row


def _qkv(x, mod, g, w_qk, w_vt, g_q, g_k, cos, sin, n_kv, tm=512, row_chunk=256):
    bsz, s, d = x.shape
    hd = g_q.shape[-1]
    n_heads = d // hd
    group = n_heads // n_kv
    kernel = functools.partial(_qkv_kernel, q_scale=math.log2(math.e) / math.sqrt(hd),
                               row_chunk=row_chunk)
    return pl.pallas_call(
        kernel,
        grid=(bsz, s // tm),
        in_specs=[
            pl.BlockSpec((None, tm, d), lambda b, i: (b, i, 0)),
            pl.BlockSpec((None, 6, d), lambda b, i: (b, 0, 0)),
            pl.BlockSpec((1, d), lambda b, i: (0, 0)),
            pl.BlockSpec(w_qk.shape, lambda b, i: (0, 0)),
            pl.BlockSpec(w_vt.shape, lambda b, i: (0, 0)),
            pl.BlockSpec((1, hd), lambda b, i: (0, 0)),
            pl.BlockSpec((1, hd), lambda b, i: (0, 0)),
            pl.BlockSpec((tm, hd), lambda b, i: (i, 0)),
            pl.BlockSpec((tm, hd), lambda b, i: (i, 0)),
        ],
        out_specs=[
            pl.BlockSpec((None, n_kv, group, tm, hd), lambda b, i: (b, 0, 0, i, 0)),
            pl.BlockSpec((None, n_kv, tm, hd), lambda b, i: (b, 0, i, 0)),
            pl.BlockSpec((None, n_kv, 1, hd + V7X_BF16_SUBLANES, tm), lambda b, i: (b, 0, i, 0, 0)),
        ],
        out_shape=[
            jax.ShapeDtypeStruct((bsz, n_kv, group, s, hd), BF16),
            jax.ShapeDtypeStruct((bsz, n_kv, s, hd), BF16),
            jax.ShapeDtypeStruct((bsz, n_kv, s // tm, hd + V7X_BF16_SUBLANES, tm), BF16),
        ],
        compiler_params=_compiler_params(2),
        name="qkv",
    )(x, mod, g.reshape(1, d), w_qk, w_vt, g_q.reshape(1, hd), g_k.reshape(1, hd), cos, sin)


ATTN_SLOTS = 4


def _attn_kernel(q_ref, k_ref, vt_ref, o_ref, acc_scr, *slots):
    group, tq, hd = q_ref.shape
    n_kt, _, tk = vt_ref.shape
    nq = group * tq
    s_scr, p_scr = slots[:ATTN_SLOTS], slots[ATTN_SLOTS:]
    st_idx = jnp.minimum(pl.program_id(2), 0)
    ld_idx = jnp.minimum(pl.program_id(1), 0)

    def scores(kt, s_scr):
        k_tile = k_ref[pl.ds(pl.multiple_of(kt * tk, tk), tk), :]
        s = lax.dot_general(k_tile, q_ref[...].reshape(nq, hd), (((1,), (1,)), ((), ())),
                            preferred_element_type=F32)
        s_scr[st_idx] = s
        return jnp.max(s, axis=0, keepdims=True)

    def softmax(s_scr, p_scr, m, tile_max):
        m_new = jnp.maximum(m, tile_max)
        p_scr[st_idx] = jnp.exp2(s_scr[ld_idx] - m_new).astype(BF16)
        return m_new, jnp.exp2(m - m_new)

    def values(kt, p_scr, alpha):
        pv = jnp.dot(vt_ref[kt], p_scr[ld_idx], preferred_element_type=F32)
        acc_scr[...] = alpha * acc_scr[...] + pv

    def tile(t, slot, carry, prefetch):
        m, max_t, max_t1, alpha_t2, alpha_t1 = carry
        max_t2 = scores(t + 2, s_scr[(slot + 2) % ATTN_SLOTS]) if prefetch else None
        if alpha_t2 is not None:
            values(t - 2, p_scr[(slot - 2) % ATTN_SLOTS], alpha_t2)
        m, alpha = softmax(s_scr[slot], p_scr[slot], m, max_t)
        return m, max_t1, max_t2, alpha_t1, alpha

    def group_of_tiles(t0, carry, n_prefetch=ATTN_SLOTS):
        for slot in range(ATTN_SLOTS):
            carry = tile(t0 + slot, slot, carry, prefetch=slot < n_prefetch)
        return carry

    n_groups = n_kt // ATTN_SLOTS
    acc_scr[...] = jnp.zeros_like(acc_scr)
    m0 = jnp.full((1, nq), -jnp.inf, F32)
    carry = (m0, scores(0, s_scr[0]), scores(1, s_scr[1]), None, None)
    carry = group_of_tiles(0, carry)
    carry = lax.fori_loop(1, n_groups - 1, lambda g, c: group_of_tiles(g * ATTN_SLOTS, c), carry,
                          unroll=True)
    _, _, _, alpha_t2, alpha_t1 = group_of_tiles(n_kt - ATTN_SLOTS, carry,
                                                 n_prefetch=ATTN_SLOTS - 2)
    values(n_kt - 2, p_scr[ATTN_SLOTS - 2], alpha_t2)
    values(n_kt - 1, p_scr[ATTN_SLOTS - 1], alpha_t1)
    out_t = acc_scr[:hd, :] / acc_scr[hd:hd + 1, :]
    for g in range(group):
        o_ref[:, g * hd:(g + 1) * hd] = out_t[:, g * tq:(g + 1) * tq].T.astype(BF16)


def _attn_bounded_kernel(q_ref, k_ref, vt_ref, o_ref, acc_scr, *p_scr):
    group, tq_step, hd = q_ref.shape
    n_kt, _, tk = vt_ref.shape
    n_blocks = acc_scr.shape[0]
    tq = tq_step // n_blocks
    nq = group * tq
    st_idx = jnp.minimum(pl.program_id(2), 0)
    ld_idx = jnp.minimum(pl.program_id(1), 0)

    for blk in range(n_blocks):
        q_rows = slice(blk * tq, (blk + 1) * tq)
        acc = acc_scr.at[blk]

        def probabilities(kt):
            k_tile = k_ref[kt * tk:(kt + 1) * tk, :]
            s = lax.dot_general(k_tile, q_ref[:, q_rows, :].reshape(nq, hd),
                                (((1,), (1,)), ((), ())), preferred_element_type=F32)
            p = jnp.exp2(s)
            p_scr[kt % ATTN_SLOTS][st_idx] = p.astype(BF16)
            return jnp.sum(p, axis=0, keepdims=True)

        def values(kt):
            acc[...] += jnp.dot(vt_ref[kt, :hd, :], p_scr[kt % ATTN_SLOTS][ld_idx],
                                preferred_element_type=F32)

        acc[...] = jnp.zeros_like(acc)
        l = probabilities(0) + probabilities(1)
        for kt in range(n_kt):
            if kt + 2 < n_kt:
                l = l + probabilities(kt + 2)
            values(kt)
        out_t = acc[...] / l
        for g in range(group):
            o_ref[q_rows, g * hd:(g + 1) * hd] = out_t[:, g * tq:(g + 1) * tq].T.astype(BF16)


MAX_UNSTABILISED_SCORE = 60.0


def _attention(q, k, vt, score_bound, tq=128, bounded_blocks=4):
    bsz, n_kv, group, s, hd = q.shape
    n_kt, tk = vt.shape[2], vt.shape[4]
    assert n_kt % ATTN_SLOTS == 0 and n_kt >= 2 * ATTN_SLOTS
    nq = group * tq

    def call(body, scratch, name, tq_step):
        return pl.pallas_call(
            body,
            grid=(bsz, n_kv, s // tq_step),
            in_specs=[
                pl.BlockSpec((None, None, group, tq_step, hd), lambda b, h, i: (b, h, 0, i, 0)),
                pl.BlockSpec((None, None, s, hd), lambda b, h, i: (b, h, 0, 0)),
                pl.BlockSpec((None, None, n_kt, vt.shape[3], tk), lambda b, h, i: (b, h, 0, 0, 0)),
            ],
            out_specs=pl.BlockSpec((None, tq_step, group * hd), lambda b, h, i: (b, i, h)),
            out_shape=jax.ShapeDtypeStruct((bsz, s, n_kv * group * hd), BF16),
            scratch_shapes=scratch,
            compiler_params=_compiler_params(3),
            name=name,
        )(q, k, vt)

    p_slots = [pltpu.VMEM((1, tk, nq), BF16)] * ATTN_SLOTS
    s_slots = [pltpu.VMEM((1, tk, nq), F32)] * ATTN_SLOTS
    return lax.cond(
        score_bound <= MAX_UNSTABILISED_SCORE,
        lambda: call(_attn_bounded_kernel, [pltpu.VMEM((bounded_blocks, hd, nq), F32)] + p_slots,
                     "attention_bounded", bounded_blocks * tq),
        lambda: call(_attn_kernel, [pltpu.VMEM((vt.shape[3], nq), F32)] + s_slots + p_slots,
                     "attention", tq),
    )


def _oproj_kernel(a_ref, w_ref, x_ref, mod_ref, o_ref):
    y = jnp.dot(a_ref[...], w_ref[...], preferred_element_type=F32)
    o_ref[...] = x_ref[...] + mod_ref[2:3, :] * y


def _oproj(a, w_o, x, mod, tm=512):
    bsz, s, d = x.shape
    return pl.pallas_call(
        _oproj_kernel,
        grid=(bsz, s // tm),
        in_specs=[
            pl.BlockSpec((None, tm, a.shape[-1]), lambda b, i: (b, i, 0)),
            pl.BlockSpec(w_o.shape, lambda b, i: (0, 0)),
            pl.BlockSpec((None, tm, d), lambda b, i: (b, i, 0)),
            pl.BlockSpec((None, 6, d), lambda b, i: (b, 0, 0)),
        ],
        out_specs=pl.BlockSpec((None, tm, d), lambda b, i: (b, i, 0)),
        out_shape=jax.ShapeDtypeStruct((bsz, s, d), F32),
        compiler_params=_compiler_params(2),
        name="oproj",
    )(a, w_o, x, mod)


def _ffn_kernel(x_ref, xp_ref, xn_ref, mod_ref, g_ref, wg_ref, wv_ref, cwg_ref, cwv_ref,
                cbg_ref, cbv_ref, wd_ref, gf_ref, o_ref, h_scr, ag_scr, av_scr, *, final_norm, n_chunks):
    tm = x_ref.shape[0]
    halo = xp_ref.shape[0]
    i = pl.program_id(1)
    j = pl.program_id(2)

    rc = tm // n_chunks
    edges = [0] + [c * rc + 2 * halo for c in range(1, n_chunks)] + [tm + 2 * halo]

    def normed_rows(lo, hi):
        gain, scale, shift = g_ref[...], mod_ref[4:5, :], mod_ref[3:4, :]
        parts = []
        if lo == 0:
            hp = _modnorm(xp_ref[...], gain, scale, shift)
            parts.append(jnp.where(i == 0, 0.0, hp))
        x_lo, x_hi = max(lo - halo, 0), min(hi - halo, tm)
        parts.append(_modnorm(x_ref[x_lo:x_hi, :], gain, scale, shift))
        if hi == tm + 2 * halo:
            hn = _modnorm(xn_ref[...], gain, scale, shift)
            parts.append(jnp.where(i == pl.num_programs(1) - 1, 0.0, hn))
        return jnp.concatenate(parts, axis=0).astype(BF16)

    def conv(a_scr, cw_ref, cb_ref, row0):
        base = halo + row0
        return (cw_ref[0:1, :] * a_scr[base - 1:base - 1 + rc, :]
                + cw_ref[1:2, :] * a_scr[base:base + rc, :]
                + cw_ref[2:3, :] * a_scr[base + 1:base + 1 + rc, :]
                + cb_ref[...])

    def step(first):
        for c in range(n_chunks):
            rows = slice(edges[c], edges[c + 1])
            if first:
                h_scr[rows, :] = normed_rows(edges[c], edges[c + 1])
            ag_scr[rows, :] = jnp.dot(h_scr[rows, :], wg_ref[...], preferred_element_type=F32)
            av_scr[rows, :] = jnp.dot(h_scr[rows, :], wv_ref[...], preferred_element_type=F32)
        for c in range(n_chunks):
            act = (_gelu(conv(ag_scr, cwg_ref, cbg_ref, c * rc))
                   * conv(av_scr, cwv_ref, cbv_ref, c * rc)).astype(BF16)
            y = jnp.dot(act, wd_ref[...], preferred_element_type=F32)
            out_rows = slice(c * rc, (c + 1) * rc)
            if first:
                o_ref[out_rows, :] = y
            else:
                o_ref[out_rows, :] += y

    pl.when(j == 0)(lambda: step(True))
    pl.when(j > 0)(lambda: step(False))

    @pl.when(j == pl.num_programs(2) - 1)
    def _():
        y = x_ref[...] + mod_ref[5:6, :] * o_ref[...]
        if final_norm:
            y = (y * _rms_scale(y)) * gf_ref[...]
        o_ref[...] = y


def _ffn(x, mod, g, layer, w_up, conv_w, conv_b, w_down, g_final, final_norm,
         tm=512, tf=512, row_chunks=2):
    bsz, s, d = x.shape
    dff = w_down.shape[1]
    nj = dff // tf
    halo = V7X_SUBLANES
    nhb = tm // halo
    last_hb = s // halo - 1
    kernel = functools.partial(_ffn_kernel, final_norm=final_norm, n_chunks=row_chunks)
    return pl.pallas_call(
        kernel,
        grid=(bsz, s // tm, nj),
        in_specs=[
            pl.BlockSpec((None, tm, d), lambda b, i, j: (b, i, 0)),
            pl.BlockSpec((None, halo, d), lambda b, i, j: (b, jnp.maximum(i * nhb - 1, 0), 0)),
            pl.BlockSpec((None, halo, d), lambda b, i, j: (b, jnp.minimum((i + 1) * nhb, last_hb), 0)),
            pl.BlockSpec((None, 6, d), lambda b, i, j: (b, 0, 0)),
            pl.BlockSpec((1, d), lambda b, i, j: (0, 0)),
            pl.BlockSpec((None, d, tf), lambda b, i, j: (layer, 0, j)),
            pl.BlockSpec((None, d, tf), lambda b, i, j: (layer, 0, nj + j)),
            pl.BlockSpec((None, 3, tf), lambda b, i, j: (layer, 0, j)),
            pl.BlockSpec((None, 3, tf), lambda b, i, j: (layer, 0, nj + j)),
            pl.BlockSpec((None, 1, tf), lambda b, i, j: (layer, 0, j)),
            pl.BlockSpec((None, 1, tf), lambda b, i, j: (layer, 0, nj + j)),
            pl.BlockSpec((None, tf, d), lambda b, i, j: (layer, j, 0)),
            pl.BlockSpec((1, d), lambda b, i, j: (0, 0)),
        ],
        out_specs=pl.BlockSpec((None, tm, d), lambda b, i, j: (b, i, 0)),
        out_shape=jax.ShapeDtypeStruct((bsz, s, d), F32),
        scratch_shapes=[
            pltpu.VMEM((tm + 2 * halo, d), BF16),
            pltpu.VMEM((tm + 2 * halo, tf), F32),
            pltpu.VMEM((tm + 2 * halo, tf), F32),
        ],
        compiler_params=_compiler_params(3),
        name="ffn",
    )(x, x, x, mod, g.reshape(1, d), w_up, w_up, conv_w, conv_w,
      conv_b[:, None, :], conv_b[:, None, :], w_down, g_final.reshape(1, d))


def _rope_tables(s, hd):
    axis_dim = hd // 2
    n_rows = s // GRID_W
    inv_freq = ROPE_THETA ** (-jnp.arange(0, axis_dim, 2, dtype=F32) / axis_dim)
    ang_r = jnp.arange(n_rows).astype(F32)[:, None] * inv_freq[None, :]
    ang_c = jnp.arange(GRID_W).astype(F32)[:, None] * inv_freq[None, :]
    zr, zc = jnp.zeros_like(ang_r), jnp.zeros_like(ang_c)
    cos_r = jnp.concatenate([jnp.cos(ang_r)] * 2 + [zr] * 2, axis=-1)
    sin_r = jnp.concatenate([-jnp.sin(ang_r), jnp.sin(ang_r), zr, zr], axis=-1)
    cos_c = jnp.concatenate([zc] * 2 + [jnp.cos(ang_c)] * 2, axis=-1)
    sin_c = jnp.concatenate([zc, zc, -jnp.sin(ang_c), jnp.sin(ang_c)], axis=-1)
    expand = lambda by_row, by_col: (by_row[:, None, :] + by_col[None, :, :]).reshape(s, hd)
    return expand(cos_r, cos_c), expand(sin_r, sin_c)


def kernel(x, c, w_ada, b_ada, g_norm, g_final, a_w_in, a_g_v, a_w_s, a_b_s, a_w_out,
           b_w_qkv, b_g_q, b_g_k, b_w_o, f_w_up, f_conv_w, f_conv_b, f_w_down):
    depth = w_ada.shape[0]
    bsz, s, d = x.shape
    hd = b_g_q.shape[-1]
    n_heads = d // hd
    n_kv = (b_w_qkv.shape[-1] // hd - n_heads) // 2
    n_mixers = 2

    mod = _adaln(c, w_ada, b_ada)
    w_up_bf16 = f_w_up.astype(BF16)
    w_down_bf16 = f_w_down.astype(BF16)
    cos, sin = _rope_tables(s, hd)

    for i in range(depth):
        j = i // n_mixers
        if i % n_mixers == 0:
            n_groups, chunk, _ = a_w_s.shape[1:]
            width = a_w_out.shape[1]
            bs_full = jnp.repeat(jnp.transpose(a_b_s[j]), width // n_groups, axis=1)
            x = _gmlp(x, mod[i], g_norm[i, 0], a_w_in[j].astype(BF16), a_g_v[j],
                      a_w_s[j].astype(BF16), bs_full, a_w_out[j].astype(BF16))
        else:
            w = b_w_qkv[j]
            qk_cols = (n_heads + n_kv) * hd
            q, k, vt = _qkv(x, mod[i], g_norm[i, 0], w.astype(BF16),
                            jnp.transpose(w[:, qk_cols:]).astype(BF16), b_g_q[j], b_g_k[j],
                            cos, sin, n_kv)
            score_bound = (1.01 * math.log2(math.e) * math.sqrt(hd)
                           * jnp.max(jnp.abs(b_g_q[j])) * jnp.max(jnp.abs(b_g_k[j])))
            a = _attention(q, k, vt, score_bound)
            x = _oproj(a, b_w_o[j].astype(BF16), x, mod[i])
        x = _ffn(x, mod[i], g_norm[i, 1], i, w_up_bf16, f_conv_w, f_conv_b, w_down_bf16,
                 g_final, final_norm=(i == depth - 1))
    return x
```

```python
import functools
import math

import jax
import jax.numpy as jnp
from jax import lax
from jax.experimental import pallas as pl
from jax.experimental.pallas import tpu as pltpu

EPS = 1e-6
GRID_W = 64
ROPE_THETA = 10000.0
V7X_VMEM_BYTES = 64 * 1024 * 1024
V7X_SUBLANES = 8
V7X_BF16_SUBLANES = 16
VMEM_LIMIT_BYTES = V7X_VMEM_BYTES - 8 * 1024 * 1024

BF16 = jnp.bfloat16
F32 = jnp.float32


def _compiler_params(n_grid_axes):
    return pltpu.CompilerParams(
        dimension_semantics=("arbitrary",) * n_grid_axes,
        vmem_limit_bytes=VMEM_LIMIT_BYTES,
    )


def _rms_scale(x):
    return lax.rsqrt(jnp.mean(x * x, axis=-1, keepdims=True) + EPS)


def _modnorm(x, gain, scale, shift):
    return (x * _rms_scale(x)) * (gain * (1.0 + scale)) + shift


def _gelu(x):
    return 0.5 * x * (1.0 + lax.erf(x * (1.0 / math.sqrt(2.0))))


def _adaln_kernel(c_ref, w_ref, b_ref, o_ref):
    c = c_ref[...]
    cond = c * (1.0 / (1.0 + jnp.exp(-c)))
    o_ref[...] = jnp.dot(cond, w_ref[...], precision=lax.Precision.HIGHEST,
                         preferred_element_type=F32) + b_ref[...]


def _adaln(c, w_ada, b_ada, tn=1024):
    depth, d, n = w_ada.shape
    bsz = c.shape[0]
    rows = -(-bsz // V7X_SUBLANES) * V7X_SUBLANES
    c_pad = jnp.zeros((rows, d), F32).at[:bsz].set(c)
    out = pl.pallas_call(
        _adaln_kernel,
        grid=(depth, n // tn),
        in_specs=[
            pl.BlockSpec((rows, d), lambda l, j: (0, 0)),
            pl.BlockSpec((None, d, tn), lambda l, j: (l, 0, j)),
            pl.BlockSpec((None, 1, tn), lambda l, j: (l, 0, j)),
        ],
        out_specs=pl.BlockSpec((None, rows, tn), lambda l, j: (l, 0, j)),
        out_shape=jax.ShapeDtypeStruct((depth, rows, n), F32),
        compiler_params=_compiler_params(2),
        name="adaln",
    )(c_pad, w_ada, b_ada.reshape(depth, 1, n))
    return out[:, :bsz].reshape(depth, bsz, 6, d)


def _gmlp_in_kernel(x_ref, mod_ref, g_ref, w_ref, gv_ref, u_ref, vn_ref, h_scr, v_scr, *, nc):
    width = u_ref.shape[-1]
    h_scr[...] = _modnorm(x_ref[...], g_ref[...], mod_ref[1:2, :], mod_ref[0:1, :]).astype(BF16)
    ssq = jnp.zeros((x_ref.shape[0], 1), F32)
    for c in range(width // nc):
        cols = slice(c * nc, (c + 1) * nc)
        zu = _gelu(jnp.dot(h_scr[...], w_ref[:, cols], preferred_element_type=F32))
        u_ref[:, cols] = zu.astype(BF16)
        zv = _gelu(jnp.dot(h_scr[...], w_ref[:, width + c * nc: width + (c + 1) * nc],
                           preferred_element_type=F32))
        v_scr[:, cols] = zv
        ssq = ssq + jnp.sum(zv * zv, axis=-1, keepdims=True)
    r = lax.rsqrt(ssq * (1.0 / width) + EPS)
    vn_ref[...] = ((v_scr[...] * r) * gv_ref[...]).astype(BF16)


def _gmlp_in(x, mod, g, w_in, g_v, tm=512, nc=512):
    bsz, s, d = x.shape
    width = w_in.shape[1] // 2
    kernel = functools.partial(_gmlp_in_kernel, nc=nc)
    return pl.pallas_call(
        kernel,
        grid=(bsz, s // tm),
        in_specs=[
            pl.BlockSpec((None, tm, d), lambda b, i: (b, i, 0)),
            pl.BlockSpec((None, 6, d), lambda b, i: (b, 0, 0)),
            pl.BlockSpec((1, d), lambda b, i: (0, 0)),
            pl.BlockSpec((d, 2 * width), lambda b, i: (0, 0)),
            pl.BlockSpec((1, width), lambda b, i: (0, 0)),
        ],
        out_specs=[
            pl.BlockSpec((None, tm, width), lambda b, i: (b, i, 0)),
            pl.BlockSpec((None, tm, width), lambda b, i: (b, i, 0)),
        ],
        out_shape=[jax.ShapeDtypeStruct((bsz, s, width), BF16)] * 2,
        scratch_shapes=[pltpu.VMEM((tm, d), BF16), pltpu.VMEM((tm, width), F32)],
        compiler_params=_compiler_params(2),
        name="gmlp_in",
    )(x, mod, g.reshape(1, d), w_in, g_v.reshape(1, width))


def _gmlp_out_kernel(u_ref, vn_ref, ws_ref, bs_ref, x_ref, mod_ref, wo_ref, o_ref, y_scr):
    n_groups, chunk, _ = ws_ref.shape
    gd = u_ref.shape[-1] // n_groups
    n_chunks = u_ref.shape[0] // chunk
    for g in range(n_groups):
        cols = slice(g * gd, (g + 1) * gd)
        v_all = jnp.concatenate([vn_ref[n * chunk:(n + 1) * chunk, cols] for n in range(n_chunks)],
                                axis=1)
        sv_all = jnp.dot(ws_ref[g], v_all, preferred_element_type=F32)
        for n in range(n_chunks):
            rows = slice(n * chunk, (n + 1) * chunk)
            sv = sv_all[:, n * gd:(n + 1) * gd] + bs_ref[:, cols]
            y_scr[rows, cols] = (u_ref[rows, cols].astype(F32) * sv).astype(BF16)
    y = jnp.dot(y_scr[...], wo_ref[...], preferred_element_type=F32)
    o_ref[...] = x_ref[...] + mod_ref[2:3, :] * y


def _gmlp_out(u, vn, w_s, bs_full, x, mod, w_out, tm=512):
    bsz, s, d = x.shape
    width = u.shape[-1]
    n_groups, chunk, _ = w_s.shape
    return pl.pallas_call(
        _gmlp_out_kernel,
        grid=(bsz, s // tm),
        in_specs=[
            pl.BlockSpec((None, tm, width), lambda b, i: (b, i, 0)),
            pl.BlockSpec((None, tm, width), lambda b, i: (b, i, 0)),
            pl.BlockSpec((n_groups, chunk, chunk), lambda b, i: (0, 0, 0)),
            pl.BlockSpec((chunk, width), lambda b, i: (0, 0)),
            pl.BlockSpec((None, tm, d), lambda b, i: (b, i, 0)),
            pl.BlockSpec((None, 6, d), lambda b, i: (b, 0, 0)),
            pl.BlockSpec((width, d), lambda b, i: (0, 0)),
        ],
        out_specs=pl.BlockSpec((None, tm, d), lambda b, i: (b, i, 0)),
        out_shape=jax.ShapeDtypeStruct((bsz, s, d), F32),
        scratch_shapes=[pltpu.VMEM((tm, width), BF16)],
        compiler_params=_compiler_params(2),
        name="gmlp_out",
    )(u, vn, w_s, bs_full, x, mod, w_out)


def _rope(t, cos, sin_signed, half_idx):
    quarter = t.shape[-1] // 4
    partner = jnp.where(half_idx, pltpu.roll(t, t.shape[-1] - quarter, 1), pltpu.roll(t, quarter, 1))
    return t * cos + partner * sin_signed


def _qkv_kernel(x_ref, mod_ref, g_ref, wqk_ref, wvt_ref, gq_ref, gk_ref, cos_ref, sin_ref,
                q_ref, k_ref, vt_ref, *, q_scale, row_chunk):
    n_kv, group, tm, hd = q_ref.shape
    lane = lax.broadcasted_iota(jnp.int32, (row_chunk, hd), 1)
    half_idx = (lane % (hd // 2)) < (hd // 4)
    pad_rows = vt_ref.shape[2] - hd
    ones_row = (lax.broadcasted_iota(jnp.int32, (pad_rows, row_chunk), 0) == 0).astype(BF16)

    for r0 in range(0, tm, row_chunk):
        rows = slice(r0, r0 + row_chunk)
        h = _modnorm(x_ref[rows, :], g_ref[...], mod_ref[1:2, :], mod_ref[0:1, :]).astype(BF16)
        cos = cos_ref[rows, :]
        sin = sin_ref[rows, :]

        def head(t, gain, post):
            t = t * lax.rsqrt(jnp.mean(t * t, axis=-1, keepdims=True) + EPS) * gain
            return (_rope(t, cos, sin, half_idx) * post).astype(BF16)

        for kv in range(n_kv):
            cols = slice(kv * group * hd, (kv + 1) * group * hd)
            tq = jnp.dot(h, wqk_ref[:, cols], preferred_element_type=F32)
            for g in range(group):
                q_ref[kv, g, rows, :] = head(tq[:, g * hd:(g + 1) * hd], gq_ref[...], q_scale)
        k_off = n_kv * group * hd
        tk = jnp.dot(h, wqk_ref[:, k_off:k_off + n_kv * hd], preferred_element_type=F32)
        for kv in range(n_kv):
            k_ref[kv, rows, :] = head(tk[:, kv * hd:(kv + 1) * hd], gk_ref[...], 1.0)
        vt = lax.dot_general(wvt_ref[...], h, (((1,), (1,)), ((), ())),
                             preferred_element_type=F32)
        for kv in range(n_kv):
            vt_ref[kv, 0, :hd, rows] = vt[kv * hd:(kv + 1) * hd, :].astype(BF16)
            vt_ref[kv, 0, hd:, rows] = ones_row


def _qkv(x, mod, g, w_qk, w_vt, g_q, g_k, cos, sin, n_kv, tm=512, row_chunk=256):
    bsz, s, d = x.shape
    hd = g_q.shape[-1]
    n_heads = d // hd
    group = n_heads // n_kv
    kernel = functools.partial(_qkv_kernel, q_scale=math.log2(math.e) / math.sqrt(hd),
                               row_chunk=row_chunk)
    return pl.pallas_call(
        kernel,
        grid=(bsz, s // tm),
        in_specs=[
            pl.BlockSpec((None, tm, d), lambda b, i: (b, i, 0)),
            pl.BlockSpec((None, 6, d), lambda b, i: (b, 0, 0)),
            pl.BlockSpec((1, d), lambda b, i: (0, 0)),
            pl.BlockSpec(w_qk.shape, lambda b, i: (0, 0)),
            pl.BlockSpec(w_vt.shape, lambda b, i: (0, 0)),
            pl.BlockSpec((1, hd), lambda b, i: (0, 0)),
            pl.BlockSpec((1, hd), lambda b, i: (0, 0)),
            pl.BlockSpec((tm, hd), lambda b, i: (i, 0)),
            pl.BlockSpec((tm, hd), lambda b, i: (i, 0)),
        ],
        out_specs=[
            pl.BlockSpec((None, n_kv, group, tm, hd), lambda b, i: (b, 0, 0, i, 0)),
            pl.BlockSpec((None, n_kv, tm, hd), lambda b, i: (b, 0, i, 0)),
            pl.BlockSpec((None, n_kv, 1, hd + V7X_BF16_SUBLANES, tm), lambda b, i: (b, 0, i, 0, 0)),
        ],
        out_shape=[
            jax.ShapeDtypeStruct((bsz, n_kv, group, s, hd), BF16),
            jax.ShapeDtypeStruct((bsz, n_kv, s, hd), BF16),
            jax.ShapeDtypeStruct((bsz, n_kv, s // tm, hd + V7X_BF16_SUBLANES, tm), BF16),
        ],
        compiler_params=_compiler_params(2),
        name="qkv",
    )(x, mod, g.reshape(1, d), w_qk, w_vt, g_q.reshape(1, hd), g_k.reshape(1, hd), cos, sin)


ATTN_SLOTS = 4


def _attn_kernel(q_ref, k_ref, vt_ref, o_ref, acc_scr, *slots):
    group, tq, hd = q_ref.shape
    n_kt, _, tk = vt_ref.shape
    nq = group * tq
    s_scr, p_scr = slots[:ATTN_SLOTS], slots[ATTN_SLOTS:]
    st_idx = jnp.minimum(pl.program_id(2), 0)
    ld_idx = jnp.minimum(pl.program_id(1), 0)

    def scores(kt, s_scr):
        k_tile = k_ref[pl.ds(pl.multiple_of(kt * tk, tk), tk), :]
        s = lax.dot_general(k_tile, q_ref[...].reshape(nq, hd), (((1,), (1,)), ((), ())),
                            preferred_element_type=F32)
        s_scr[st_idx] = s
        return jnp.max(s, axis=0, keepdims=True)

    def softmax(s_scr, p_scr, m, tile_max):
        m_new = jnp.maximum(m, tile_max)
        p_scr[st_idx] = jnp.exp2(s_scr[ld_idx] - m_new).astype(BF16)
        return m_new, jnp.exp2(m - m_new)

    def values(kt, p_scr, alpha):
        pv = jnp.dot(vt_ref[kt], p_scr[ld_idx], preferred_element_type=F32)
        acc_scr[...] = alpha * acc_scr[...] + pv

    def tile(t, slot, carry, prefetch):
        m, max_t, max_t1, alpha_t2, alpha_t1 = carry
        max_t2 = scores(t + 2, s_scr[(slot + 2) % ATTN_SLOTS]) if prefetch else None
        if alpha_t2 is not None:
            values(t - 2, p_scr[(slot - 2) % ATTN_SLOTS], alpha_t2)
        m, alpha = softmax(s_scr[slot], p_scr[slot], m, max_t)
        return m, max_t1, max_t2, alpha_t1, alpha

    def group_of_tiles(t0, carry, n_prefetch=ATTN_SLOTS):
        for slot in range(ATTN_SLOTS):
            carry = tile(t0 + slot, slot, carry, prefetch=slot < n_prefetch)
        return carry

    n_groups = n_kt // ATTN_SLOTS
    acc_scr[...] = jnp.zeros_like(acc_scr)
    m0 = jnp.full((1, nq), -jnp.inf, F32)
    carry = (m0, scores(0, s_scr[0]), scores(1, s_scr[1]), None, None)
    carry = group_of_tiles(0, carry)
    carry = lax.fori_loop(1, n_groups - 1, lambda g, c: group_of_tiles(g * ATTN_SLOTS, c), carry,
                          unroll=True)
    _, _, _, alpha_t2, alpha_t1 = group_of_tiles(n_kt - ATTN_SLOTS, carry,
                                                 n_prefetch=ATTN_SLOTS - 2)
    values(n_kt - 2, p_scr[ATTN_SLOTS - 2], alpha_t2)
    values(n_kt - 1, p_scr[ATTN_SLOTS - 1], alpha_t1)
    out_t = acc_scr[:hd, :] / acc_scr[hd:hd + 1, :]
    for g in range(group):
        o_ref[:, g * hd:(g + 1) * hd] = out_t[:, g * tq:(g + 1) * tq].T.astype(BF16)


def _attn_bounded_kernel(q_ref, k_ref, vt_ref, o_ref, acc_scr, *p_scr):
    group, tq_step, hd = q_ref.shape
    n_kt, _, tk = vt_ref.shape
    n_blocks = acc_scr.shape[0]
    tq = tq_step // n_blocks
    nq = group * tq
    st_idx = jnp.minimum(pl.program_id(2), 0)
    ld_idx = jnp.minimum(pl.program_id(1), 0)

    for blk in range(n_blocks):
        q_rows = slice(blk * tq, (blk + 1) * tq)
        acc = acc_scr.at[blk]

        def probabilities(kt):
            k_tile = k_ref[kt * tk:(kt + 1) * tk, :]
            s = lax.dot_general(k_tile, q_ref[:, q_rows, :].reshape(nq, hd),
                                (((1,), (1,)), ((), ())), preferred_element_type=F32)
            p = jnp.exp2(s)
            p_scr[kt % ATTN_SLOTS][st_idx] = p.astype(BF16)
            return jnp.sum(p, axis=0, keepdims=True)

        def values(kt):
            acc[...] += jnp.dot(vt_ref[kt, :hd, :], p_scr[kt % ATTN_SLOTS][ld_idx],
                                preferred_element_type=F32)

        acc[...] = jnp.zeros_like(acc)
        l = probabilities(0) + probabilities(1)
        for kt in range(n_kt):
            if kt + 2 < n_kt:
                l = l + probabilities(kt + 2)
            values(kt)
        out_t = acc[...] / l
        for g in range(group):
            o_ref[q_rows, g * hd:(g + 1) * hd] = out_t[:, g * tq:(g + 1) * tq].T.astype(BF16)


MAX_UNSTABILISED_SCORE = 60.0


def _attention(q, k, vt, score_bound, tq=128, bounded_blocks=4):
    bsz, n_kv, group, s, hd = q.shape
    n_kt, tk = vt.shape[2], vt.shape[4]
    assert n_kt % ATTN_SLOTS == 0 and n_kt >= 2 * ATTN_SLOTS
    nq = group * tq

    def call(body, scratch, name, tq_step):
        return pl.pallas_call(
            body,
            grid=(bsz, n_kv, s // tq_step),
            in_specs=[
                pl.BlockSpec((None, None, group, tq_step, hd), lambda b, h, i: (b, h, 0, i, 0)),
                pl.BlockSpec((None, None, s, hd), lambda b, h, i: (b, h, 0, 0)),
                pl.BlockSpec((None, None, n_kt, vt.shape[3], tk), lambda b, h, i: (b, h, 0, 0, 0)),
            ],
            out_specs=pl.BlockSpec((None, tq_step, group * hd), lambda b, h, i: (b, i, h)),
            out_shape=jax.ShapeDtypeStruct((bsz, s, n_kv * group * hd), BF16),
            scratch_shapes=scratch,
            compiler_params=_compiler_params(3),
            name=name,
        )(q, k, vt)

    p_slots = [pltpu.VMEM((1, tk, nq), BF16)] * ATTN_SLOTS
    s_slots = [pltpu.VMEM((1, tk, nq), F32)] * ATTN_SLOTS
    return lax.cond(
        score_bound <= MAX_UNSTABILISED_SCORE,
        lambda: call(_attn_bounded_kernel, [pltpu.VMEM((bounded_blocks, hd, nq), F32)] + p_slots,
                     "attention_bounded", bounded_blocks * tq),
        lambda: call(_attn_kernel, [pltpu.VMEM((vt.shape[3], nq), F32)] + s_slots + p_slots,
                     "attention", tq),
    )


def _oproj_kernel(a_ref, w_ref, x_ref, mod_ref, o_ref):
    y = jnp.dot(a_ref[...], w_ref[...], preferred_element_type=F32)
    o_ref[...] = x_ref[...] + mod_ref[2:3, :] * y


def _oproj(a, w_o, x, mod, tm=512):
    bsz, s, d = x.shape
    return pl.pallas_call(
        _oproj_kernel,
        grid=(bsz, s // tm),
        in_specs=[
            pl.BlockSpec((None, tm, a.shape[-1]), lambda b, i: (b, i, 0)),
            pl.BlockSpec(w_o.shape, lambda b, i: (0, 0)),
            pl.BlockSpec((None, tm, d), lambda b, i: (b, i, 0)),
            pl.BlockSpec((None, 6, d), lambda b, i: (b, 0, 0)),
        ],
        out_specs=pl.BlockSpec((None, tm, d), lambda b, i: (b, i, 0)),
        out_shape=jax.ShapeDtypeStruct((bsz, s, d), F32),
        compiler_params=_compiler_params(2),
        name="oproj",
    )(a, w_o, x, mod)


def _ffn_kernel(x_ref, xp_ref, xn_ref, mod_ref, g_ref, wg_ref, wv_ref, cwg_ref, cwv_ref,
                cbg_ref, cbv_ref, wd_ref, gf_ref, o_ref, h_scr, ag_scr, av_scr, *, final_norm, n_chunks):
    tm = x_ref.shape[0]
    halo = xp_ref.shape[0]
    i = pl.program_id(1)
    j = pl.program_id(2)

    rc = tm // n_chunks
    edges = [0] + [c * rc + 2 * halo for c in range(1, n_chunks)] + [tm + 2 * halo]

    def normed_rows(lo, hi):
        gain, scale, shift = g_ref[...], mod_ref[4:5, :], mod_ref[3:4, :]
        parts = []
        if lo == 0:
            hp = _modnorm(xp_ref[...], gain, scale, shift)
            parts.append(jnp.where(i == 0, 0.0, hp))
        x_lo, x_hi = max(lo - halo, 0), min(hi - halo, tm)
        parts.append(_modnorm(x_ref[x_lo:x_hi, :], gain, scale, shift))
        if hi == tm + 2 * halo:
            hn = _modnorm(xn_ref[...], gain, scale, shift)
            parts.append(jnp.where(i == pl.num_programs(1) - 1, 0.0, hn))
        return jnp.concatenate(parts, axis=0).astype(BF16)

    def conv(a_scr, cw_ref, cb_ref, row0):
        base = halo + row0
        return (cw_ref[0:1, :] * a_scr[base - 1:base - 1 + rc, :]
                + cw_ref[1:2, :] * a_scr[base:base + rc, :]
                + cw_ref[2:3, :] * a_scr[base + 1:base + 1 + rc, :]
                + cb_ref[...])

    def step(first):
        for c in range(n_chunks):
            rows = slice(edges[c], edges[c + 1])
            if first:
                h_scr[rows, :] = normed_rows(edges[c], edges[c + 1])
            ag_scr[rows, :] = jnp.dot(h_scr[rows, :], wg_ref[...], preferred_element_type=F32)
            av_scr[rows, :] = jnp.dot(h_scr[rows, :], wv_ref[...], preferred_element_type=F32)
        for c in range(n_chunks):
            act = (_gelu(conv(ag_scr, cwg_ref, cbg_ref, c * rc))
                   * conv(av_scr, cwv_ref, cbv_ref, c * rc)).astype(BF16)
            y = jnp.dot(act, wd_ref[...], preferred_element_type=F32)
            out_rows = slice(c * rc, (c + 1) * rc)
            if first:
                o_ref[out_rows, :] = y
            else:
                o_ref[out_rows, :] += y

    pl.when(j == 0)(lambda: step(True))
    pl.when(j > 0)(lambda: step(False))

    @pl.when(j == pl.num_programs(2) - 1)
    def _():
        y = x_ref[...] + mod_ref[5:6, :] * o_ref[...]
        if final_norm:
            y = (y * _rms_scale(y)) * gf_ref[...]
        o_ref[...] = y


def _ffn(x, mod, g, layer, w_up, conv_w, conv_b, w_down, g_final, final_norm,
         tm=512, tf=512, row_chunks=2):
    bsz, s, d = x.shape
    dff = w_down.shape[1]
    nj = dff // tf
    halo = V7X_SUBLANES
    nhb = tm // halo
    last_hb = s // halo - 1
    kernel = functools.partial(_ffn_kernel, final_norm=final_norm, n_chunks=row_chunks)
    return pl.pallas_call(
        kernel,
        grid=(bsz, s // tm, nj),
        in_specs=[
            pl.BlockSpec((None, tm, d), lambda b, i, j: (b, i, 0)),
            pl.BlockSpec((None, halo, d), lambda b, i, j: (b, jnp.maximum(i * nhb - 1, 0), 0)),
            pl.BlockSpec((None, halo, d), lambda b, i, j: (b, jnp.minimum((i + 1) * nhb, last_hb), 0)),
            pl.BlockSpec((None, 6, d), lambda b, i, j: (b, 0, 0)),
            pl.BlockSpec((1, d), lambda b, i, j: (0, 0)),
            pl.BlockSpec((None, d, tf), lambda b, i, j: (layer, 0, j)),
            pl.BlockSpec((None, d, tf), lambda b, i, j: (layer, 0, nj + j)),
            pl.BlockSpec((None, 3, tf), lambda b, i, j: (layer, 0, j)),
            pl.BlockSpec((None, 3, tf), lambda b, i, j: (layer, 0, nj + j)),
            pl.BlockSpec((None, 1, tf), lambda b, i, j: (layer, 0, j)),
            pl.BlockSpec((None, 1, tf), lambda b, i, j: (layer, 0, nj + j)),
            pl.BlockSpec((None, tf, d), lambda b, i, j: (layer, j, 0)),
            pl.BlockSpec((1, d), lambda b, i, j: (0, 0)),
        ],
        out_specs=pl.BlockSpec((None, tm, d), lambda b, i, j: (b, i, 0)),
        out_shape=jax.ShapeDtypeStruct((bsz, s, d), F32),
        scratch_shapes=[
            pltpu.VMEM((tm + 2 * halo, d), BF16),
            pltpu.VMEM((tm + 2 * halo, tf), F32),
            pltpu.VMEM((tm + 2 * halo, tf), F32),
        ],
        compiler_params=_compiler_params(3),
        name="ffn",
    )(x, x, x, mod, g.reshape(1, d), w_up, w_up, conv_w, conv_w,
      conv_b[:, None, :], conv_b[:, None, :], w_down, g_final.reshape(1, d))


def _rope_tables(s, hd):
    axis_dim = hd // 2
    n_rows = s // GRID_W
    inv_freq = ROPE_THETA ** (-jnp.arange(0, axis_dim, 2, dtype=F32) / axis_dim)
    ang_r = jnp.arange(n_rows).astype(F32)[:, None] * inv_freq[None, :]
    ang_c = jnp.arange(GRID_W).astype(F32)[:, None] * inv_freq[None, :]
    zr, zc = jnp.zeros_like(ang_r), jnp.zeros_like(ang_c)
    cos_r = jnp.concatenate([jnp.cos(ang_r)] * 2 + [zr] * 2, axis=-1)
    sin_r = jnp.concatenate([-jnp.sin(ang_r), jnp.sin(ang_r), zr, zr], axis=-1)
    cos_c = jnp.concatenate([zc] * 2 + [jnp.cos(ang_c)] * 2, axis=-1)
    sin_c = jnp.concatenate([zc, zc, -jnp.sin(ang_c), jnp.sin(ang_c)], axis=-1)
    expand = lambda by_row, by_col: (by_row[:, None, :] + by_col[None, :, :]).reshape(s, hd)
    return expand(cos_r, cos_c), expand(sin_r, sin_c)


def kernel(x, c, w_ada, b_ada, g_norm, g_final, a_w_in, a_g_v, a_w_s, a_b_s, a_w_out,
           b_w_qkv, b_g_q, b_g_k, b_w_o, f_w_up, f_conv_w, f_conv_b, f_w_down):
    depth = w_ada.shape[0]
    bsz, s, d = x.shape
    hd = b_g_q.shape[-1]
    n_heads = d // hd
    n_kv = (b_w_qkv.shape[-1] // hd - n_heads) // 2
    n_mixers = 2

    mod = _adaln(c, w_ada, b_ada)
    w_up_bf16 = f_w_up.astype(BF16)
    w_down_bf16 = f_w_down.astype(BF16)
    cos, sin = _rope_tables(s, hd)

    for i in range(depth):
        j = i // n_mixers
        if i % n_mixers == 0:
            n_groups, chunk, _ = a_w_s.shape[1:]
            width = a_w_out.shape[1]
            bs_full = jnp.repeat(jnp.transpose(a_b_s[j]), width // n_groups, axis=1)
            u, vn = _gmlp_in(x, mod[i], g_norm[i, 0], a_w_in[j].astype(BF16), a_g_v[j])
            x = _gmlp_out(u, vn, a_w_s[j].astype(BF16), bs_full, x, mod[i], a_w_out[j].astype(BF16))
        else:
            w = b_w_qkv[j]
            qk_cols = (n_heads + n_kv) * hd
            q, k, vt = _qkv(x, mod[i], g_norm[i, 0], w.astype(BF16),
                            jnp.transpose(w[:, qk_cols:]).astype(BF16), b_g_q[j], b_g_k[j],
                            cos, sin, n_kv)
            score_bound = (1.01 * math.log2(math.e) * math.sqrt(hd)
                           * jnp.max(jnp.abs(b_g_q[j])) * jnp.max(jnp.abs(b_g_k[j])))
            a = _attention(q, k, vt, score_bound)
            x = _oproj(a, b_w_o[j].astype(BF16), x, mod[i])
        x = _ffn(x, mod[i], g_norm[i, 1], i, w_up_bf16, f_conv_w, f_conv_b, w_down_bf16,
                 g_final, final_norm=(i == depth - 1))
    return x
```
